```python
import jax, jax.numpy as jnp
from jax import lax
import numpy as np

D_MODEL = 2048
BATCH = 2
SEQ = 8192
DEPTH = 1

GRID_W = 64
CTX_LEN = 256
HEAD_DIM = 128
N_Q_HEADS = 16
N_KV_HEADS = 4
Q_PER_KV = N_Q_HEADS // N_KV_HEADS
WINDOW = 128
BLOCK = 128
ROPE_THETA = 10000.0
GLA_HEADS = 4
GLA_DK = D_MODEL // 2 // GLA_HEADS
GLA_DV = D_MODEL // GLA_HEADS
GLA_LOWRANK = 16
GLA_GATE_NORM = 16.0
GLA_CHUNK = 64
D_FF = 5632
CONV_W = 3
EPS = 1e-6
ATTN_WIDTH = N_Q_HEADS * HEAD_DIM
KV_WIDTH = N_KV_HEADS * HEAD_DIM
GLA_K_WIDTH = GLA_HEADS * GLA_DK
GLA_V_WIDTH = GLA_HEADS * GLA_DV
IN_SPLITS = (ATTN_WIDTH, KV_WIDTH, KV_WIDTH, GLA_K_WIDTH, GLA_K_WIDTH, GLA_V_WIDTH, GLA_V_WIDTH,
             GLA_LOWRANK, GLA_LOWRANK, D_MODEL, D_MODEL)
IN_WIDTH = sum(IN_SPLITS)

kernel_name = "hybrid_swa_gla_convffn_prefix_dit"


def rms_norm(x, g):
    xf = x.astype(jnp.float32)
    y = xf * lax.rsqrt(jnp.mean(xf * xf, axis=-1, keepdims=True) + EPS)
    return (y * g.astype(jnp.float32)).astype(x.dtype)


def modulate(x, g, shift, scale):
    return rms_norm(x, g) * (1 + scale) + shift


def split_heads(a, n_heads):
    return a.reshape(*a.shape[:-1], n_heads, -1)


def axial_rope_angles(n):
    rows = n // GRID_W
    row = jnp.repeat(jnp.arange(rows), GRID_W)
    col = jnp.tile(jnp.arange(GRID_W), rows)
    n_freq = HEAD_DIM // 4
    inv = ROPE_THETA ** (-jnp.arange(n_freq, dtype=jnp.float32) / n_freq)
    ang = jnp.concatenate([row[:, None] * inv, col[:, None] * inv], axis=-1)
    return jnp.cos(ang), jnp.sin(ang)


def apply_rope(x, cos, sin):
    half = HEAD_DIM // 2
    x1, x2 = x[..., :half].astype(jnp.float32), x[..., half:].astype(jnp.float32)
    c, s = cos[None, :, None, :], sin[None, :, None, :]
    return jnp.concatenate([x1 * c - x2 * s, x2 * c + x1 * s], axis=-1).astype(x.dtype)


def gqa_scores(q5, k):
    return jnp.einsum("bqhgd,bkhd->bhgqk", q5, k).astype(jnp.float32) * (HEAD_DIM ** -0.5)


def attend_with_sink(parts, sink_g):
    sink_b = sink_g[None, :, :, None]
    m = sink_b
    for s, _ in parts:
        m = jnp.maximum(m, s.max(axis=-1))
    denom = jnp.exp(sink_b - m)
    out = None
    for s, val in parts:
        p = jnp.exp(s - m[..., None])
        denom = denom + p.sum(axis=-1)
        o = jnp.einsum("bhgqk,bkhd->bqhgd", p, val.astype(jnp.float32))
        out = o if out is None else out + o
    return out / jnp.transpose(denom, (0, 3, 1, 2))[..., None]


def windowed_attention(q, k, v, k_ctx, v_ctx, sink):
    B, n = q.shape[:2]
    nb = n // BLOCK
    span = BLOCK + 2 * WINDOW
    pad = ((0, 0), (WINDOW, WINDOW), (0, 0), (0, 0))
    kp, vp = jnp.pad(k, pad), jnp.pad(v, pad)
    qb = jnp.swapaxes(q.reshape(B, nb, BLOCK, N_KV_HEADS, Q_PER_KV, HEAD_DIM), 0, 1)
    sink_g = sink.reshape(N_KV_HEADS, Q_PER_KV).astype(jnp.float32)
    s_ctx_all = None

    def block(args):
        qi, i = args
        start = i * BLOCK
        kw = lax.dynamic_slice_in_dim(kp, start, span, axis=1)
        vw = lax.dynamic_slice_in_dim(vp, start, span, axis=1)
        qpos = start + jnp.arange(BLOCK)
        kpos = start - WINDOW + jnp.arange(span)
        valid = ((jnp.abs(qpos[:, None] - kpos[None, :]) <= WINDOW)
                 & (kpos >= 0)[None, :] & (kpos < n)[None, :])
        s_lat = jnp.where(valid, gqa_scores(qi, kw), -jnp.inf)
        s_ctx = gqa_scores(qi, k_ctx)
        return attend_with_sink([(s_lat, vw), (s_ctx, v_ctx)], sink_g)

    o = lax.map(block, (qb, jnp.arange(nb)))
    return jnp.swapaxes(o, 0, 1).reshape(B, n, ATTN_WIDTH).astype(q.dtype)


def context_attention(q, k, v, sink):
    B, L = q.shape[:2]
    q5 = q.reshape(B, L, N_KV_HEADS, Q_PER_KV, HEAD_DIM)
    sink_g = sink.reshape(N_KV_HEADS, Q_PER_KV).astype(jnp.float32)
    o = attend_with_sink([(gqa_scores(q5, k), v)], sink_g)
    return o.reshape(B, L, ATTN_WIDTH).astype(q.dtype)


def gla_chunked(q, k, v, g, s0):
    B, T, H, _ = q.shape
    dv = v.shape[-1]
    nc = T // GLA_CHUNK

    def to_chunks(a):
        return jnp.transpose(a.reshape(B, nc, GLA_CHUNK, H, a.shape[-1]), (1, 0, 3, 2, 4)).astype(jnp.float32)

    causal = jnp.tril(jnp.ones((GLA_CHUNK, GLA_CHUNK), dtype=bool))[..., None]

    def step(S, inp):
        qc, kc, vc, gc = inp
        b = jnp.cumsum(gc, axis=2)
        b_last = b[:, :, -1:]
        o_inter = jnp.einsum("bhcd,bhde->bhce", qc * jnp.exp(b), S)
        rel = jnp.where(causal, b[:, :, :, None, :] - b[:, :, None, :, :], -jnp.inf)
        A = jnp.einsum("bhtd,bhsd,bhtsd->bhts", qc, kc, jnp.exp(rel))
        o = o_inter + jnp.einsum("bhts,bhse->bhte", A, vc)
        S = (jnp.exp(b_last[:, :, 0])[..., None] * S
             + jnp.einsum("bhsd,bhse->bhde", kc * jnp.exp(b_last - b), vc))
        return S, o

    S, o = lax.scan(step, s0, (to_chunks(q), to_chunks(k), to_chunks(v), to_chunks(g)))
    o = jnp.transpose(o, (1, 0, 3, 2, 4)).reshape(B, T, H, dv)
    return o.astype(v.dtype), S


def gla_final_state(k, v, g):
    b = jnp.cumsum(g.astype(jnp.float32), axis=1)
    w = jnp.exp(b[:, -1:] - b)
    return jnp.einsum("bthd,bthe->bhde", k.astype(jnp.float32) * w, v.astype(jnp.float32))


def flip(a):
    return a[:, ::-1]


def context_gla(q, k, v, gf, gb, need_out):
    if need_out:
        s0 = jnp.zeros((q.shape[0], GLA_HEADS, GLA_DK, GLA_DV), jnp.float32)
        of, sf = gla_chunked(q, k, v, gf, s0)
        ob, sb = gla_chunked(flip(q), flip(k), flip(v), flip(gb), s0)
        return sf, sb, of + flip(ob)
    return gla_final_state(k, v, gf), gla_final_state(flip(k), flip(v), flip(gb)), None


def latent_gla(q, k, v, gf, gb, sf, sb):
    of, _ = gla_chunked(q, k, v, gf, sf)
    ob, _ = gla_chunked(flip(q), flip(k), flip(v), flip(gb), sb)
    return of + flip(ob)


def project_heads(h, w_in, q_norm, k_norm, w_gate_f, b_gate_f, w_gate_b, b_gate_b):
    offs = np.cumsum(IN_SPLITS)[:-1].tolist()
    qa, ka, va, qb, kb, vb, rb, lrf, lrb, gate_a, gate_b = jnp.split(h @ w_in, offs, axis=-1)
    qa = rms_norm(split_heads(qa, N_Q_HEADS), q_norm)
    ka = rms_norm(split_heads(ka, N_KV_HEADS), k_norm)
    va = split_heads(va, N_KV_HEADS)
    qb = split_heads(qb, GLA_HEADS) * (GLA_DK ** -0.5)
    kb = split_heads(kb, GLA_HEADS)
    vb = split_heads(vb, GLA_HEADS)
    gf = split_heads(jax.nn.log_sigmoid((lrf @ w_gate_f + b_gate_f).astype(jnp.float32)) / GLA_GATE_NORM, GLA_HEADS)
    gb = split_heads(jax.nn.log_sigmoid((lrb @ w_gate_b + b_gate_b).astype(jnp.float32)) / GLA_GATE_NORM, GLA_HEADS)
    return qa, ka, va, qb, kb, vb, rb, gf, gb, gate_a, gate_b


def merge_branches(o_attn, o_gla, rb, gate_a, gate_b, gla_norm, w_attn_o, w_gla_o, w_out):
    B, T = o_attn.shape[:2]
    y_att = o_attn @ w_attn_o
    y_gla = (rms_norm(o_gla, gla_norm).reshape(B, T, GLA_V_WIDTH) * jax.nn.silu(rb)) @ w_gla_o
    return (jax.nn.sigmoid(gate_a) * y_att + jax.nn.sigmoid(gate_b) * y_gla) @ w_out


def conv_ffn(h, w_up, conv_w, conv_b, w_down):
    T = h.shape[1]
    u = h @ w_up
    up = jnp.pad(u, ((0, 0), (CONV_W // 2, CONV_W // 2), (0, 0)))
    u = sum(up[:, j:j + T] * conv_w[j] for j in range(CONV_W)) + conv_b
    a, g = jnp.split(u, 2, axis=-1)
    return (jax.nn.silu(a) * g) @ w_down


def setup_inputs(seed: int = 0) -> dict:
    key = jax.random.key(seed)
    ks = jax.random.split(key, 24)
    D, L = D_MODEL, DEPTH

    def nrm(k, shape, s):
        return jax.random.normal(k, shape, jnp.float32) * s

    return {
        "x": nrm(ks[0], (BATCH, SEQ, D), 1.0),
        "c": nrm(ks[1], (BATCH, D), 1.0),
        "ctx": nrm(ks[2], (BATCH, CTX_LEN, D), 1.0),
        "c_ctx": nrm(ks[3], (D,), 1.0),
        "w_mod": nrm(ks[4], (L, D, 6 * D), 0.5 * D ** -0.5),
        "b_mod": nrm(ks[5], (L, 6 * D), 0.02),
        "g_mix": 1.0 + nrm(ks[6], (L, D), 0.05),
        "w_in": nrm(ks[7], (L, D, IN_WIDTH), D ** -0.5),
        "q_norm": 1.0 + nrm(ks[8], (L, HEAD_DIM), 0.05),
        "k_norm": 1.0 + nrm(ks[9], (L, HEAD_DIM), 0.05),
        "attn_sink": nrm(ks[10], (L, N_Q_HEADS), 0.5),
        "w_gate_f": nrm(ks[11], (L, GLA_LOWRANK, GLA_K_WIDTH), GLA_LOWRANK ** -0.5),
        "b_gate_f": nrm(ks[12], (L, GLA_K_WIDTH), 0.1),
        "w_gate_b": nrm(ks[13], (L, GLA_LOWRANK, GLA_K_WIDTH), GLA_LOWRANK ** -0.5),
        "b_gate_b": nrm(ks[14], (L, GLA_K_WIDTH), 0.1),
        "gla_norm": 1.0 + nrm(ks[15], (L, GLA_DV), 0.05),
        "w_attn_o": nrm(ks[16], (L, ATTN_WIDTH, D), ATTN_WIDTH ** -0.5),
        "w_gla_o": nrm(ks[17], (L, GLA_V_WIDTH, D), GLA_V_WIDTH ** -0.5),
        "w_out": nrm(ks[18], (L, D, D), D ** -0.5),
        "g_ffn": 1.0 + nrm(ks[19], (L, D), 0.05),
        "w_up": nrm(ks[20], (L, D, 2 * D_FF), D ** -0.5),
        "conv_w": nrm(ks[21], (L, CONV_W, 2 * D_FF), CONV_W ** -0.5),
        "conv_b": nrm(ks[22], (L, 2 * D_FF), 0.02),
        "w_down": nrm(ks[23], (L, D_FF, D), D_FF ** -0.5),
    }


def reference(x, c, ctx, c_ctx, w_mod, b_mod, g_mix, w_in, q_norm, k_norm, attn_sink,
              w_gate_f, b_gate_f, w_gate_b, b_gate_b, gla_norm, w_attn_o, w_gla_o, w_out,
              g_ffn, w_up, conv_w, conv_b, w_down):
    n = x.shape[1]
    cos, sin = axial_rope_angles(n)
    for l in range(DEPTH):
        last = l == DEPTH - 1
        mod_x = jnp.split((jax.nn.silu(c) @ w_mod[l] + b_mod[l])[:, None, :], 6, axis=-1)
        mod_c = jnp.split((jax.nn.silu(c_ctx) @ w_mod[l] + b_mod[l])[None, None, :], 6, axis=-1)
        proj = lambda h: project_heads(h, w_in[l], q_norm[l], k_norm[l],
                                       w_gate_f[l], b_gate_f[l], w_gate_b[l], b_gate_b[l])

        qa, ka, va, qb, kb, vb, rb, gf, gb, gate_a, gate_b = proj(modulate(x, g_mix[l], mod_x[0], mod_x[1]))
        qa, ka = apply_rope(qa, cos, sin), apply_rope(ka, cos, sin)
        cqa, cka, cva, cqb, ckb, cvb, crb, cgf, cgb, cgate_a, cgate_b = proj(
            modulate(ctx, g_mix[l], mod_c[0], mod_c[1]))

        o_attn = windowed_attention(qa, ka, va, cka, cva, attn_sink[l])
        sf, sb, o_gla_c = context_gla(cqb, ckb, cvb, cgf, cgb, need_out=not last)
        o_gla = latent_gla(qb, kb, vb, gf, gb, sf, sb)
        x = x + mod_x[2] * merge_branches(o_attn, o_gla, rb, gate_a, gate_b, gla_norm[l],
                                          w_attn_o[l], w_gla_o[l], w_out[l])
        x = x + mod_x[5] * conv_ffn(modulate(x, g_ffn[l], mod_x[3], mod_x[4]),
                                    w_up[l], conv_w[l], conv_b[l], w_down[l])

        if not last:
            o_attn_c = context_attention(cqa, cka, cva, attn_sink[l])
            ctx = ctx + mod_c[2] * merge_branches(o_attn_c, o_gla_c, crb, cgate_a, cgate_b, gla_norm[l],
                                                  w_attn_o[l], w_gla_o[l], w_out[l])
            ctx = ctx + mod_c[5] * conv_ffn(modulate(ctx, g_ffn[l], mod_c[3], mod_c[4]),
                                            w_up[l], conv_w[l], conv_b[l], w_down[l])
    return x
```

```python
import functools

import numpy as np
import jax
import jax.numpy as jnp
from jax import lax
from jax.experimental import pallas as pl
from jax.experimental.pallas import tpu as pltpu

F32 = jnp.float32
BF16 = jnp.bfloat16

D_MODEL = 2048
GRID_W = 64
HEAD_DIM = 128
N_Q_HEADS = 16
N_KV_HEADS = 4
Q_PER_KV = N_Q_HEADS // N_KV_HEADS
WINDOW = 128
ROPE_THETA = 10000.0
GLA_HEADS = 4
GLA_DK = D_MODEL // 2 // GLA_HEADS
GLA_DV = D_MODEL // GLA_HEADS
GLA_LOWRANK = 16
GLA_GATE_NORM = 16.0
D_FF = 5632
EPS = 1e-6
ATTN_WIDTH = N_Q_HEADS * HEAD_DIM
KV_WIDTH = N_KV_HEADS * HEAD_DIM
GLA_K_WIDTH = GLA_HEADS * GLA_DK
GLA_V_WIDTH = GLA_HEADS * GLA_DV

LANES = 128
VMEM_LIMIT_BYTES = 56 * 1024 * 1024

GLA_CHUNK = 128
GLA_SUB = 16
GLA_LEVELS = (64, 32, 16)

P_QB, P_KB, P_VB, P_RB, P_GA, P_GB, P_VA = 0, 1024, 2048, 4096, 6144, 8192, 10240
P_WIDTH = 10752
QK_WIDTH = ATTN_WIDTH + KV_WIDTH


def _params(*sem):
    return pltpu.CompilerParams(dimension_semantics=sem, vmem_limit_bytes=VMEM_LIMIT_BYTES)


def _sigmoid(x):
    return 1.0 / (1.0 + jnp.exp(-x))


def _mod_kernel(a_ref, w_ref, b_ref, o_ref):
    a = a_ref[...]
    s = (a * _sigmoid(a)).astype(BF16)
    o_ref[...] = jnp.dot(s, w_ref[...].astype(BF16), preferred_element_type=F32) + b_ref[...]


def _modulation(cc, w_mod, b_mod):
    d, n = w_mod.shape
    tn = 1024
    return pl.pallas_call(
        _mod_kernel,
        grid=(n // tn,),
        in_specs=[pl.BlockSpec((8, d), lambda j: (0, 0)),
                  pl.BlockSpec((d, tn), lambda j: (0, j)),
                  pl.BlockSpec((1, tn), lambda j: (0, j))],
        out_specs=pl.BlockSpec((8, tn), lambda j: (0, j)),
        out_shape=jax.ShapeDtypeStruct((8, n), F32),
        compiler_params=_params("parallel"),
        name="modulation",
    )(cc, w_mod, b_mod.reshape(1, n))


def _norm_mm_kernel(*refs, mode, rope, has_lr, tn):
    it = iter(refs)
    x_ref, shift_ref, scale_ref, gain_ref, w_ref, cs_ref = [next(it) for _ in range(6)]
    cos_ref = sin_ref = wlr_ref = lr_ref = None
    if rope:
        cos_ref, sin_ref = next(it), next(it)
    if has_lr:
        wlr_ref = next(it)
    o_ref = next(it)
    if has_lr:
        lr_ref = next(it)
    h_ref = next(it)

    @pl.when(pl.program_id(1) == 0)
    def _():
        x = x_ref[...]
        ms = jnp.mean(x * x, axis=-1, keepdims=True)
        y = x * lax.rsqrt(ms + EPS) * gain_ref[...]
        h = (y * (1.0 + scale_ref[0]) + shift_ref[0]).astype(BF16)
        h_ref[...] = h
        if has_lr:
            lr_ref[...] = jnp.dot(h, wlr_ref[...], preferred_element_type=F32)

    acc = jnp.dot(h_ref[...], w_ref[...], preferred_element_type=F32)
    if mode == "plain":
        o_ref[...] = (acc * cs_ref[...]).astype(o_ref.dtype)
    else:
        for hh in range(tn // HEAD_DIM):
            sl = slice(hh * HEAD_DIM, (hh + 1) * HEAD_DIM)
            a = acc[:, sl]
            ms = jnp.mean(a * a, axis=-1, keepdims=True)
            a = a * lax.rsqrt(ms + EPS) * cs_ref[:, sl]
            if rope:
                a = a * cos_ref[...] + pltpu.roll(a, HEAD_DIM // 2, axis=1) * sin_ref[...]
            o_ref[:, sl] = a.astype(o_ref.dtype)


def _norm_matmul(x2, shift, scale, gain, w, colvec, *, mode, rows_per_batch, cos2=None, sin2=None,
                 w_lr=None, tm=1024, tn=512):
    m, d = x2.shape
    n = w.shape[1]
    tm = min(tm, rows_per_batch)
    tpb = rows_per_batch // tm
    rope = cos2 is not None
    has_lr = w_lr is not None
    in_specs = [pl.BlockSpec((tm, d), lambda i, j: (i, 0)),
                pl.BlockSpec((1, 1, d), lambda i, j: (i // tpb, 0, 0)),
                pl.BlockSpec((1, 1, d), lambda i, j: (i // tpb, 0, 0)),
                pl.BlockSpec((1, d), lambda i, j: (0, 0)),
                pl.BlockSpec((d, tn), lambda i, j: (0, j)),
                pl.BlockSpec((1, tn), lambda i, j: (0, j))]
    args = [x2, shift, scale, gain.reshape(1, d), w, colvec.reshape(1, n)]
    if rope:
        in_specs += [pl.BlockSpec((tm, HEAD_DIM), lambda i, j: (i % tpb, 0))] * 2
        args += [cos2, sin2]
    out_shape = [jax.ShapeDtypeStruct((m, n), BF16)]
    out_specs = [pl.BlockSpec((tm, tn), lambda i, j: (i, j))]
    if has_lr:
        in_specs.append(pl.BlockSpec((d, LANES), lambda i, j: (0, 0)))
        args.append(w_lr)
        out_shape.append(jax.ShapeDtypeStruct((m, LANES), F32))
        out_specs.append(pl.BlockSpec((tm, LANES), lambda i, j: (i, 0)))
    res = pl.pallas_call(
        functools.partial(_norm_mm_kernel, mode=mode, rope=rope, has_lr=has_lr, tn=tn),
        grid=(m // tm, n // tn),
        in_specs=in_specs,
        out_specs=out_specs,
        out_shape=out_shape,
        scratch_shapes=[pltpu.VMEM((tm, d), BF16)],
        compiler_params=_params("parallel", "arbitrary"),
        name="norm_matmul_" + mode,
    )(*args)
    return res if has_lr else res[0]


def _attn_kernel(sink_ref, q_ref, kp_ref, kc_ref, kn_ref, kx_ref, vp_ref, vc_ref, vn_ref, vx_ref,
                 o_ref, *, nblk):
    h = pl.program_id(1)
    i = pl.program_id(2)
    blk = HEAD_DIM
    q = q_ref[...]
    qs = jnp.concatenate([q[:, g * HEAD_DIM:(g + 1) * HEAD_DIM] for g in range(Q_PER_KV)], axis=0)
    dn = (((1,), (1,)), ((), ()))
    sc = HEAD_DIM ** -0.5

    def scores(k_ref):
        return lax.dot_general(qs, k_ref[...], dn, preferred_element_type=F32) * sc

    row = lax.broadcasted_iota(jnp.int32, (blk, blk), 0)
    col = lax.broadcasted_iota(jnp.int32, (blk, blk), 1)
    ninf = jnp.float32(-jnp.inf)
    bias_p = jnp.where(col >= row, 0.0, ninf) + jnp.where(i > 0, 0.0, ninf)
    bias_n = jnp.where(col <= row, 0.0, ninf) + jnp.where(i < nblk - 1, 0.0, ninf)
    bias_p = jnp.concatenate([bias_p] * Q_PER_KV, axis=0)
    bias_n = jnp.concatenate([bias_n] * Q_PER_KV, axis=0)

    s_p = scores(kp_ref) + bias_p
    s_c = scores(kc_ref)
    s_n = scores(kn_ref) + bias_n
    s_x = scores(kx_ref)
    sink = jnp.concatenate(
        [jnp.full((blk, 1), sink_ref[h * Q_PER_KV + g], F32) for g in range(Q_PER_KV)], axis=0)
    m = jnp.maximum(
        jnp.maximum(jnp.max(s_p, axis=-1, keepdims=True), jnp.max(s_c, axis=-1, keepdims=True)),
        jnp.maximum(jnp.max(s_n, axis=-1, keepdims=True), jnp.max(s_x, axis=-1, keepdims=True)))
    m = jnp.maximum(m, sink)
    p_p = jnp.exp(s_p - m)
    p_c = jnp.exp(s_c - m)
    p_n = jnp.exp(s_n - m)
    p_x = jnp.exp(s_x - m)
    denom = (jnp.exp(sink - m) + jnp.sum(p_p, axis=-1, keepdims=True)
             + jnp.sum(p_c, axis=-1, keepdims=True) + jnp.sum(p_n, axis=-1, keepdims=True)
             + jnp.sum(p_x, axis=-1, keepdims=True))
    o = (jnp.dot(p_p.astype(BF16), vp_ref[...], preferred_element_type=F32)
         + jnp.dot(p_c.astype(BF16), vc_ref[...], preferred_element_type=F32)
         + jnp.dot(p_n.astype(BF16), vn_ref[...], preferred_element_type=F32)
         + jnp.dot(p_x.astype(BF16), vx_ref[...], preferred_element_type=F32))
    o = o / denom
    for g in range(Q_PER_KV):
        o_ref[:, g * HEAD_DIM:(g + 1) * HEAD_DIM] = o[g * blk:(g + 1) * blk].astype(o_ref.dtype)


def _attention(qk, p, qk_c, p_c, sink, batch, seq, ctx_len):
    blk = HEAD_DIM
    nblk = seq // blk
    kcol = ATTN_WIDTH // HEAD_DIM
    vcol = P_VA // HEAD_DIM

    def kv_spec(col0, shift):
        def imap(b, h, i):
            return (b * nblk + jnp.clip(i + shift, 0, nblk - 1), col0 + h)
        return pl.BlockSpec((blk, HEAD_DIM), imap)

    in_specs = [pl.BlockSpec(memory_space=pltpu.SMEM),
                pl.BlockSpec((blk, Q_PER_KV * HEAD_DIM), lambda b, h, i: (b * nblk + i, h)),
                kv_spec(kcol, -1), kv_spec(kcol, 0), kv_spec(kcol, 1),
                pl.BlockSpec((ctx_len, HEAD_DIM), lambda b, h, i: (b, kcol + h)),
                kv_spec(vcol, -1), kv_spec(vcol, 0), kv_spec(vcol, 1),
                pl.BlockSpec((ctx_len, HEAD_DIM), lambda b, h, i: (b, vcol + h))]
    return pl.pallas_call(
        functools.partial(_attn_kernel, nblk=nblk),
        grid=(batch, N_KV_HEADS, nblk),
        in_specs=in_specs,
        out_specs=pl.BlockSpec((blk, Q_PER_KV * HEAD_DIM), lambda b, h, i: (b * nblk + i, h)),
        out_shape=jax.ShapeDtypeStruct((batch * seq, ATTN_WIDTH), BF16),
        compiler_params=_params("parallel", "parallel", "arbitrary"),
        name="window_attention",
    )(sink, qk, qk, qk, qk, qk_c, p, p, p, p_c)


def _gla_masks():
    c = GLA_CHUNK
    t = np.arange(c)[:, None]
    s = np.arange(c)[None, :]
    fwd = []
    for w in GLA_LEVELS:
        fwd.append((t // (2 * w) == s // (2 * w)) & (t % (2 * w) >= w) & (s % (2 * w) < w))
    fwd.append((t // GLA_SUB == s // GLA_SUB) & (s <= t))
    fwd = np.stack(fwd).astype(np.float32)
    return np.stack([fwd, np.transpose(fwd, (0, 2, 1))])


def _gla_chunk(q_ref, k_ref, v_ref, lr_ref, wg_ref, bg_ref, mask_ref, st_ref, w_ref, o_ref, reverse):
    c = GLA_CHUNK
    q = q_ref[...].astype(F32)
    k = k_ref[...].astype(F32)
    v = v_ref[...]
    xg = jnp.dot(lr_ref[...], wg_ref[...], preferred_element_type=F32,
                 precision=lax.Precision.HIGHEST) + bg_ref[...]
    g = (jnp.minimum(xg, 0.0) - jnp.log1p(jnp.exp(-jnp.abs(xg)))) * (1.0 / GLA_GATE_NORM)

    row = lax.broadcasted_iota(jnp.int32, (c, GLA_DK), 0)
    b = g
    sh = 1
    while sh < c:
        if reverse:
            b = b + jnp.where(row < c - sh, pltpu.roll(b, c - sh, axis=0), 0.0)
        else:
            b = b + jnp.where(row >= sh, pltpu.roll(b, sh, axis=0), 0.0)
        sh *= 2

    dn_nt = (((1,), (1,)), ((), ()))
    a_mat = jnp.zeros((c, c), F32)
    for lvl, w in enumerate(GLA_LEVELS):
        pieces = []
        for mblk in range(c // (2 * w)):
            r = mblk * 2 * w + (w if reverse else w - 1)
            pieces.append(jnp.broadcast_to(b[r:r + 1, :], (2 * w, GLA_DK)))
        ref = jnp.concatenate(pieces, axis=0) if len(pieces) > 1 else pieces[0]
        qn = (q * jnp.exp(jnp.minimum(b - ref, 0.0))).astype(BF16)
        kn = (k * jnp.exp(jnp.minimum(ref - b, 0.0))).astype(BF16)
        a_mat = a_mat + lax.dot_general(qn, kn, dn_nt, preferred_element_type=F32) * mask_ref[lvl]

    for i in range(c // GLA_SUB):
        r0 = i * GLA_SUB
        qi = q[r0:r0 + GLA_SUB]
        ki = k[r0:r0 + GLA_SUB]
        bi = b[r0:r0 + GLA_SUB]
        for s in range(GLA_SUB):
            e = jnp.exp(jnp.minimum(bi - bi[s:s + 1, :], 0.0))
            w_ref[pl.ds((r0 + s) * GLA_SUB, GLA_SUB), :] = (qi * e * ki[s:s + 1, :]).astype(BF16)
    red = jnp.dot(w_ref[...], jnp.ones((GLA_DK, LANES), BF16), preferred_element_type=F32)
    lane = lax.broadcasted_iota(jnp.int32, (GLA_SUB, LANES), 1)
    rows = []
    for i in range(c // GLA_SUB):
        blk = jnp.zeros((GLA_SUB, LANES), F32)
        for s in range(GLA_SUB):
            col = i * GLA_SUB + s
            blk = jnp.where(lane == col, red[col * GLA_SUB:(col + 1) * GLA_SUB], blk)
        rows.append(blk)
    a_mat = a_mat + jnp.concatenate(rows, axis=0) * mask_ref[len(GLA_LEVELS)]

    st = st_ref[...]
    qe = (q * jnp.exp(b)).astype(BF16)
    o = lax.dot_general(qe, st.astype(BF16), dn_nt, preferred_element_type=F32)
    o = o + jnp.dot(a_mat.astype(BF16), v, preferred_element_type=F32)
    o_ref[...] = o.astype(o_ref.dtype)

    r_end = 0 if reverse else c - 1
    b_end = b[r_end:r_end + 1, :]
    ke = (k * jnp.exp(b_end - b)).astype(BF16)
    upd = lax.dot_general(v, ke, (((0,), (0,)), ((), ())), preferred_element_type=F32)
    st_ref[...] = st * jnp.exp(b_end) + upd


def _gla_kernel(qf_ref, kf_ref, vf_ref, lrf_ref, qb_ref, kb_ref, vb_ref, lrb_ref,
                wgf_ref, bgf_ref, wgb_ref, bgb_ref, mask_ref, s0f_ref, s0b_ref,
                of_ref, ob_ref, sf_ref, sb_ref, stf_ref, stb_ref, wf_ref, wb_ref):
    cidx = pl.program_id(2)

    @pl.when(cidx == 0)
    def _():
        stf_ref[...] = s0f_ref[0, 0]
        stb_ref[...] = s0b_ref[0, 0]

    _gla_chunk(qf_ref, kf_ref, vf_ref, lrf_ref, wgf_ref, bgf_ref, mask_ref.at[0], stf_ref, wf_ref,
               of_ref, False)
    _gla_chunk(qb_ref, kb_ref, vb_ref, lrb_ref, wgb_ref, bgb_ref, mask_ref.at[1], stb_ref, wb_ref,
               ob_ref, True)

    @pl.when(cidx == pl.num_programs(2) - 1)
    def _():
        sf_ref[0, 0] = stf_ref[...]
        sb_ref[0, 0] = stb_ref[...]


def _gla(p, lr, wgf, bgf, wgb, bgb, s0f, s0b, batch, seq):
    c = GLA_CHUNK
    nc = seq // c
    kcol = P_KB // GLA_DK
    vcol = P_VB // GLA_DV

    def fwd(b, h, i):
        return b * nc + i

    def bwd(b, h, i):
        return b * nc + (nc - 1 - i)

    def data_specs(rowf):
        return [pl.BlockSpec((c, GLA_DK), lambda b, h, i: (rowf(b, h, i), h)),
                pl.BlockSpec((c, GLA_DK), lambda b, h, i: (rowf(b, h, i), kcol + h)),
                pl.BlockSpec((c, GLA_DV), lambda b, h, i: (rowf(b, h, i), vcol + h)),
                pl.BlockSpec((c, LANES), lambda b, h, i: (rowf(b, h, i), 0))]

    gate_specs = [pl.BlockSpec((LANES, GLA_DK), lambda b, h, i: (0, h)),
                  pl.BlockSpec((1, GLA_DK), lambda b, h, i: (0, h))]
    state_spec = pl.BlockSpec((1, 1, GLA_DV, GLA_DK), lambda b, h, i: (b, h, 0, 0))
    masks = jnp.asarray(_gla_masks())
    in_specs = (data_specs(fwd) + data_specs(bwd) + gate_specs + gate_specs
                + [pl.BlockSpec(masks.shape, lambda b, h, i: (0, 0, 0, 0)), state_spec, state_spec])
    out_specs = [pl.BlockSpec((c, GLA_DV), lambda b, h, i: (fwd(b, h, i), h)),
                 pl.BlockSpec((c, GLA_DV), lambda b, h, i: (bwd(b, h, i), h)),
                 state_spec, state_spec]
    out_shape = [jax.ShapeDtypeStruct((batch * seq, GLA_V_WIDTH), BF16)] * 2 + [
        jax.ShapeDtypeStruct((batch, GLA_HEADS, GLA_DV, GLA_DK), F32)] * 2
    return pl.pallas_call(
        _gla_kernel,
        grid=(batch, GLA_HEADS, nc),
        in_specs=in_specs,
        out_specs=out_specs,
        out_shape=out_shape,
        scratch_shapes=[pltpu.VMEM((GLA_DV, GLA_DK), F32), pltpu.VMEM((GLA_DV, GLA_DK), F32),
                        pltpu.VMEM((c * GLA_SUB, GLA_DK), BF16), pltpu.VMEM((c * GLA_SUB, GLA_DK), BF16)],
        compiler_params=_params("parallel", "parallel", "arbitrary"),
        name="gla_scan",
    )(p, p, p, lr, p, p, p, lr, wgf, bgf, wgb, bgb, masks, s0f, s0b)


def _merge_kernel(oa_ref, of_ref, ob_ref, rb_ref, ga_ref, gb_ref, gn_ref, wa_ref, wg_ref, z_ref):
    og = of_ref[...].astype(F32) + ob_ref[...].astype(F32)
    parts = []
    for hh in range(GLA_HEADS):
        a = og[:, hh * GLA_DV:(hh + 1) * GLA_DV]
        ms = jnp.mean(a * a, axis=-1, keepdims=True)
        parts.append(a * lax.rsqrt(ms + EPS) * gn_ref[...])
    n = jnp.concatenate(parts, axis=1)
    rb = rb_ref[...].astype(F32)
    n = (n * (rb * _sigmoid(rb))).astype(BF16)
    y_gla = jnp.dot(n, wg_ref[...], preferred_element_type=F32)
    y_att = jnp.dot(oa_ref[...], wa_ref[...], preferred_element_type=F32)
    z = _sigmoid(ga_ref[...].astype(F32)) * y_att + _sigmoid(gb_ref[...].astype(F32)) * y_gla
    z_ref[...] = z.astype(z_ref.dtype)


def _merge(o_attn, o_f, o_b, p, gla_norm, w_attn_o, w_gla_o, tm=256):
    m, d = o_attn.shape
    row = lambda i: (i, 0)
    const = lambda i: (0, 0)
    wspec = pl.BlockSpec((d, d), const, pipeline_mode=pl.Buffered(1))
    return pl.pallas_call(
        _merge_kernel,
        grid=(m // tm,),
        in_specs=[pl.BlockSpec((tm, d), row), pl.BlockSpec((tm, d), row), pl.BlockSpec((tm, d), row),
                  pl.BlockSpec((tm, d), lambda i: (i, P_RB // d)),
                  pl.BlockSpec((tm, d), lambda i: (i, P_GA // d)),
                  pl.BlockSpec((tm, d), lambda i: (i, P_GB // d)),
                  pl.BlockSpec((1, GLA_DV), const), wspec, wspec],
        out_specs=pl.BlockSpec((tm, d), row),
        out_shape=jax.ShapeDtypeStruct((m, d), BF16),
        compiler_params=_params("parallel"),
        name="merge_gates",
    )(o_attn, o_f, o_b, p, p, p, gla_norm.reshape(1, GLA_DV), w_attn_o, w_gla_o)


def _out_proj_kernel(z_ref, w_ref, x_ref, gate_ref, o_ref):
    y = jnp.dot(z_ref[...], w_ref[...], preferred_element_type=F32)
    o_ref[...] = x_ref[...] + gate_ref[0] * y


def _out_proj(z, w_out, x2, gate, rows_per_batch, tm=512):
    m, d = x2.shape
    tm = min(tm, rows_per_batch)
    tpb = rows_per_batch // tm
    return pl.pallas_call(
        _out_proj_kernel,
        grid=(m // tm,),
        in_specs=[pl.BlockSpec((tm, d), lambda i: (i, 0)),
                  pl.BlockSpec((d, d), lambda i: (0, 0), pipeline_mode=pl.Buffered(1)),
                  pl.BlockSpec((tm, d), lambda i: (i, 0)),
                  pl.BlockSpec((1, 1, d), lambda i: (i // tpb, 0, 0))],
        out_specs=pl.BlockSpec((tm, d), lambda i: (i, 0)),
        out_shape=jax.ShapeDtypeStruct((m, d), F32),
        compiler_params=_params("parallel"),
        name="out_proj_residual",
    )(z, w_out, x2, gate)


def _ffn_down_kernel(ua_ref, uap_ref, uan_ref, ug_ref, ugp_ref, ugn_ref, cwa_ref, cwg_ref, cba_ref,
                     cbg_ref, wd_ref, x_ref, gate_ref, o_ref, acc_ref, *, tpb, halo):
    i = pl.program_id(0)
    kk = pl.program_id(1)
    tm = ua_ref.shape[0]
    keep_prev = jnp.where((i % tpb) == 0, 0.0, 1.0)
    keep_next = jnp.where((i % tpb) == tpb - 1, 0.0, 1.0)
    rowi = lax.broadcasted_iota(jnp.int32, ua_ref.shape, 0)

    def conv(u_ref, up_ref, un_ref, cw_ref, cb_ref):
        u = u_ref[...].astype(F32)
        prev_row = up_ref[halo - 1:halo, :].astype(F32) * keep_prev
        next_row = un_ref[0:1, :].astype(F32) * keep_next
        u_prev = jnp.where(rowi == 0, prev_row, pltpu.roll(u, 1, axis=0))
        u_next = jnp.where(rowi == tm - 1, next_row, pltpu.roll(u, tm - 1, axis=0))
        return cw_ref[0:1, :] * u_prev + cw_ref[1:2, :] * u + cw_ref[2:3, :] * u_next + cb_ref[...]

    a = conv(ua_ref, uap_ref, uan_ref, cwa_ref, cba_ref)
    g = conv(ug_ref, ugp_ref, ugn_ref, cwg_ref, cbg_ref)
    act = (a * _sigmoid(a) * g).astype(BF16)
    contrib = jnp.dot(act, wd_ref[...], preferred_element_type=F32)

    @pl.when(kk == 0)
    def _():
        acc_ref[...] = contrib

    @pl.when(kk > 0)
    def _():
        acc_ref[...] += contrib

    @pl.when(kk == pl.num_programs(1) - 1)
    def _():
        o_ref[...] = x_ref[...] + gate_ref[0] * acc_ref[...]


def _ffn_down(u, conv_w, conv_b, w_down, x2, gate, rows_per_batch, tm=512, tk=512):
    m, d = x2.shape
    dff = w_down.shape[0]
    tm = min(tm, rows_per_batch)
    tpb = rows_per_batch // tm
    halo = 16
    hb = tm // halo
    nk = dff // tk
    nhalo = m // halo

    def main(off):
        return pl.BlockSpec((tm, tk), lambda i, k: (i, off + k))

    def prev(off):
        return pl.BlockSpec((halo, tk), lambda i, k: (jnp.maximum(i * hb - 1, 0), off + k))

    def nxt(off):
        return pl.BlockSpec((halo, tk), lambda i, k: (jnp.minimum((i + 1) * hb, nhalo - 1), off + k))

    def vec(rows, off):
        return pl.BlockSpec((rows, tk), lambda i, k: (0, off + k))

    return pl.pallas_call(
        functools.partial(_ffn_down_kernel, tpb=tpb, halo=halo),
        grid=(m // tm, nk),
        in_specs=[main(0), prev(0), nxt(0), main(nk), prev(nk), nxt(nk),
                  vec(3, 0), vec(3, nk), vec(1, 0), vec(1, nk),
                  pl.BlockSpec((tk, d), lambda i, k: (k, 0)),
                  pl.BlockSpec((tm, d), lambda i, k: (i, 0)),
                  pl.BlockSpec((1, 1, d), lambda i, k: (i // tpb, 0, 0))],
        out_specs=pl.BlockSpec((tm, d), lambda i, k: (i, 0)),
        out_shape=jax.ShapeDtypeStruct((m, d), F32),
        scratch_shapes=[pltpu.VMEM((tm, d), F32)],
        compiler_params=_params("parallel", "arbitrary"),
        name="convffn_down",
    )(u, u, u, u, u, u, conv_w, conv_w, conv_b.reshape(1, 2 * dff), conv_b.reshape(1, 2 * dff),
      w_down, x2, gate)


def _rope_tables(n):
    rows = n // GRID_W
    row = jnp.repeat(jnp.arange(rows), GRID_W)
    col = jnp.tile(jnp.arange(GRID_W), rows)
    n_freq = HEAD_DIM // 4
    inv = ROPE_THETA ** (-jnp.arange(n_freq, dtype=F32) / n_freq)
    ang = jnp.concatenate([row[:, None] * inv, col[:, None] * inv], axis=-1)
    cos, sin = jnp.cos(ang), jnp.sin(ang)
    return jnp.concatenate([cos, cos], axis=-1), jnp.concatenate([-sin, sin], axis=-1)


def _split_w_in(w_in):
    splits = (ATTN_WIDTH, KV_WIDTH, KV_WIDTH, GLA_K_WIDTH, GLA_K_WIDTH, GLA_V_WIDTH, GLA_V_WIDTH,
              GLA_LOWRANK, GLA_LOWRANK, D_MODEL, D_MODEL)
    offs = np.cumsum(splits)[:-1].tolist()
    qa, ka, va, qb, kb, vb, rb, lrf, lrb, ga, gb = jnp.split(w_in, offs, axis=-1)
    w_qk = jnp.concatenate([qa, ka], axis=-1).astype(BF16)
    w_p = jnp.concatenate([qb, kb, vb, rb, ga, gb, va], axis=-1).astype(BF16)
    pad = jnp.zeros((w_in.shape[0], LANES - 2 * GLA_LOWRANK), w_in.dtype)
    w_lr = jnp.concatenate([lrf, lrb, pad], axis=-1).astype(BF16)
    return w_qk, w_p, w_lr


def kernel(x, c, ctx, c_ctx, w_mod, b_mod, g_mix, w_in, q_norm, k_norm, attn_sink, w_gate_f, b_gate_f,
           w_gate_b, b_gate_b, gla_norm, w_attn_o, w_gla_o, w_out, g_ffn, w_up, conv_w, conv_b, w_down):
    batch, seq, d = x.shape
    ctx_len = ctx.shape[1]
    assert w_mod.shape[0] == 1, "single-layer kernel"
    assert d == D_MODEL and seq % GLA_CHUNK == 0 and ctx_len % GLA_CHUNK == 0 and batch <= 7

    cc = jnp.zeros((8, d), F32).at[:batch].set(c).at[batch].set(c_ctx)
    mod = _modulation(cc, w_mod[0], b_mod[0]).reshape(8, 6, d)
    mod_x = [mod[:batch, jj][:, None, :] for jj in range(6)]
    mod_c = [mod[batch:batch + 1, jj][:, None, :] for jj in range(6)]

    w_qk, w_p, w_lr = _split_w_in(w_in[0])
    qk_norm_w = jnp.concatenate([jnp.tile(q_norm[0], N_Q_HEADS), jnp.tile(k_norm[0], N_KV_HEADS)])
    p_scale = jnp.ones((P_WIDTH,), F32).at[P_QB:P_QB + GLA_K_WIDTH].set(GLA_DK ** -0.5)
    cos2, sin2 = _rope_tables(seq)

    x2 = x.reshape(batch * seq, d)
    c2 = ctx.reshape(batch * ctx_len, d)
    qk, lr = _norm_matmul(x2, mod_x[0], mod_x[1], g_mix[0], w_qk, qk_norm_w, mode="qk",
                          rows_per_batch=seq, cos2=cos2, sin2=sin2, w_lr=w_lr)
    p = _norm_matmul(x2, mod_x[0], mod_x[1], g_mix[0], w_p, p_scale, mode="plain", rows_per_batch=seq)
    qk_c, lr_c = _norm_matmul(c2, mod_c[0], mod_c[1], g_mix[0], w_qk, qk_norm_w, mode="qk",
                              rows_per_batch=batch * ctx_len, w_lr=w_lr)
    p_c = _norm_matmul(c2, mod_c[0], mod_c[1], g_mix[0], w_p, p_scale, mode="plain",
                       rows_per_batch=batch * ctx_len)

    o_attn = _attention(qk, p, qk_c, p_c, attn_sink[0], batch, seq, ctx_len)

    wgf = jnp.zeros((LANES, GLA_K_WIDTH), F32).at[:GLA_LOWRANK].set(w_gate_f[0])
    wgb = jnp.zeros((LANES, GLA_K_WIDTH), F32).at[GLA_LOWRANK:2 * GLA_LOWRANK].set(w_gate_b[0])
    bgf = b_gate_f[0].reshape(1, GLA_K_WIDTH)
    bgb = b_gate_b[0].reshape(1, GLA_K_WIDTH)
    zero_state = jnp.zeros((batch, GLA_HEADS, GLA_DV, GLA_DK), F32)
    _, _, sf, sb = _gla(p_c, lr_c, wgf, bgf, wgb, bgb, zero_state, zero_state, batch, ctx_len)
    o_f, o_b, _, _ = _gla(p, lr, wgf, bgf, wgb, bgb, sf, sb, batch, seq)

    z = _merge(o_attn, o_f, o_b, p, gla_norm[0], w_attn_o[0].astype(BF16), w_gla_o[0].astype(BF16))
    x1 = _out_proj(z, w_out[0].astype(BF16), x2, mod_x[2], seq)

    ones_ff = jnp.ones((2 * D_FF,), F32)
    u = _norm_matmul(x1, mod_x[3], mod_x[4], g_ffn[0], w_up[0].astype(BF16), ones_ff, mode="plain",
                     rows_per_batch=seq)
    out = _ffn_down(u, conv_w[0], conv_b[0], w_down[0].astype(BF16), x1, mod_x[5], seq)
    return out.reshape(batch, seq, d)
```

```python
import functools

import numpy as np
import jax
import jax.numpy as jnp
from jax import lax
from jax.experimental import pallas as pl
from jax.experimental.pallas import tpu as pltpu

F32 = jnp.float32
BF16 = jnp.bfloat16

D_MODEL = 2048
GRID_W = 64
HEAD_DIM = 128
N_Q_HEADS = 16
N_KV_HEADS = 4
Q_PER_KV = N_Q_HEADS // N_KV_HEADS
WINDOW = 128
ROPE_THETA = 10000.0
GLA_HEADS = 4
GLA_DK = D_MODEL // 2 // GLA_HEADS
GLA_DV = D_MODEL // GLA_HEADS
GLA_LOWRANK = 16
GLA_GATE_NORM = 16.0
D_FF = 5632
EPS = 1e-6
ATTN_WIDTH = N_Q_HEADS * HEAD_DIM
KV_WIDTH = N_KV_HEADS * HEAD_DIM
GLA_K_WIDTH = GLA_HEADS * GLA_DK
GLA_V_WIDTH = GLA_HEADS * GLA_DV

LANES = 128
MXU_WIDTH = 256
VMEM_LIMIT_BYTES = 56 * 1024 * 1024

GLA_CHUNK = 128
GLA_SUB = 8
GLA_LEVELS = (64, 32, 16, 8)
LOG2E = 1.4426950408889634
GLA_HEADS_PER_STEP = 4

P_QB, P_KB, P_VB, P_RB, P_GA, P_GB, P_VA = 0, 1024, 2048, 4096, 6144, 8192, 10240
P_WIDTH = 10752
QK_WIDTH = ATTN_WIDTH + KV_WIDTH
QK_GROUP = 512


def _params(*sem):
    return pltpu.CompilerParams(dimension_semantics=sem, vmem_limit_bytes=VMEM_LIMIT_BYTES)


def _sigmoid(x):
    return 1.0 / (1.0 + jnp.exp(-x))


def _mod_kernel(a_ref, w_ref, b_ref, o_ref):
    a = a_ref[...]
    s = (a * _sigmoid(a)).astype(BF16)
    o_ref[...] = jnp.dot(s, w_ref[...].astype(BF16), preferred_element_type=F32) + b_ref[...]


def _modulation(cc, w_mod, b_mod):
    d, n = w_mod.shape
    tn = 1024
    return pl.pallas_call(
        _mod_kernel,
        grid=(n // tn,),
        in_specs=[pl.BlockSpec((8, d), lambda j: (0, 0)),
                  pl.BlockSpec((d, tn), lambda j: (0, j)),
                  pl.BlockSpec((1, tn), lambda j: (0, j))],
        out_specs=pl.BlockSpec((8, tn), lambda j: (0, j)),
        out_shape=jax.ShapeDtypeStruct((8, n), F32),
        compiler_params=_params("parallel"),
        name="modulation",
    )(cc, w_mod, b_mod.reshape(1, n))


def _norm_mm_kernel(*refs, mode, rope, has_lr, tn):
    it = iter(refs)
    x_ref, shift_ref, scale_ref, gain_ref, w_ref, cs_ref = [next(it) for _ in range(6)]
    cos_ref = sin_ref = wlr_ref = lr_ref = seg_ref = perm_ref = None
    if mode == "qk":
        seg_ref, perm_ref = next(it), next(it)
    if rope:
        cos_ref, sin_ref = next(it), next(it)
    if has_lr:
        wlr_ref = next(it)
    o_ref = next(it)
    if has_lr:
        lr_ref = next(it)
    h_ref = next(it)

    @pl.when(pl.program_id(1) == 0)
    def _():
        x = x_ref[...]
        ms = jnp.mean(x * x, axis=-1, keepdims=True)
        y = x * lax.rsqrt(ms + EPS) * gain_ref[...]
        h = (y * (1.0 + scale_ref[0]) + shift_ref[0]).astype(BF16)
        h_ref[...] = h
        if has_lr:
            lr_ref[...] = jnp.dot(h, wlr_ref[...], preferred_element_type=F32)

    if mode == "plain":
        acc = jnp.dot(h_ref[...], w_ref[...], preferred_element_type=F32)
        o_ref[...] = (acc * cs_ref[...]).astype(o_ref.dtype)
    else:
        for gb in range(tn // QK_GROUP):
            big = jnp.dot(h_ref[...], w_ref[:, gb * QK_GROUP:(gb + 1) * QK_GROUP],
                          preferred_element_type=F32)
            for cb in range(QK_GROUP // MXU_WIDTH):
                c0 = gb * QK_GROUP + cb * MXU_WIDTH
                acc = big[:, cb * MXU_WIDTH:(cb + 1) * MXU_WIDTH]
                ss = jnp.dot((acc * acc).astype(BF16), seg_ref[...], preferred_element_type=F32)
                an = acc * lax.rsqrt(ss * (1.0 / HEAD_DIM) + EPS) * cs_ref[:, c0:c0 + MXU_WIDTH]
                if rope:
                    rot = jnp.dot(an.astype(BF16), perm_ref[...], preferred_element_type=F32)
                for hh in range(MXU_WIDTH // HEAD_DIM):
                    hs = slice(hh * HEAD_DIM, (hh + 1) * HEAD_DIM)
                    a = an[:, hs]
                    if rope:
                        a = a * cos_ref[...] + rot[:, hs] * sin_ref[...]
                    o_ref[:, c0 + hh * HEAD_DIM:c0 + (hh + 1) * HEAD_DIM] = a.astype(o_ref.dtype)


def _norm_matmul(x2, shift, scale, gain, w, colvec, *, mode, rows_per_batch, cos2=None, sin2=None,
                 w_lr=None, tm=1024, tn=512):
    m, d = x2.shape
    n = w.shape[1]
    tm = min(tm, rows_per_batch)
    tpb = rows_per_batch // tm
    rope = cos2 is not None
    has_lr = w_lr is not None
    in_specs = [pl.BlockSpec((tm, d), lambda i, j: (i, 0)),
                pl.BlockSpec((1, 1, d), lambda i, j: (i // tpb, 0, 0)),
                pl.BlockSpec((1, 1, d), lambda i, j: (i // tpb, 0, 0)),
                pl.BlockSpec((1, d), lambda i, j: (0, 0)),
                pl.BlockSpec((d, tn), lambda i, j: (0, j)),
                pl.BlockSpec((1, tn), lambda i, j: (0, j))]
    args = [x2, shift, scale, gain.reshape(1, d), w, colvec.reshape(1, n)]
    if mode == "qk":
        lane = np.arange(MXU_WIDTH)
        same_head = lane[:, None] // HEAD_DIM == lane[None, :] // HEAD_DIM
        rolled = (lane[:, None] % HEAD_DIM) == ((lane[None, :] - HEAD_DIM // 2) % HEAD_DIM)
        in_specs += [pl.BlockSpec((MXU_WIDTH, MXU_WIDTH), lambda i, j: (0, 0))] * 2
        args += [jnp.asarray(same_head, BF16), jnp.asarray(same_head & rolled, BF16)]
    if rope:
        in_specs += [pl.BlockSpec((tm, HEAD_DIM), lambda i, j: (i % tpb, 0))] * 2
        args += [cos2, sin2]
    out_shape = [jax.ShapeDtypeStruct((m, n), BF16)]
    out_specs = [pl.BlockSpec((tm, tn), lambda i, j: (i, j))]
    if has_lr:
        in_specs.append(pl.BlockSpec((d, LANES), lambda i, j: (0, 0)))
        args.append(w_lr)
        out_shape.append(jax.ShapeDtypeStruct((m, LANES), F32))
        out_specs.append(pl.BlockSpec((tm, LANES), lambda i, j: (i, 0)))
    res = pl.pallas_call(
        functools.partial(_norm_mm_kernel, mode=mode, rope=rope, has_lr=has_lr, tn=tn),
        grid=(m // tm, n // tn),
        in_specs=in_specs,
        out_specs=out_specs,
        out_shape=out_shape,
        scratch_shapes=[pltpu.VMEM((tm, d), BF16)],
        compiler_params=_params("parallel", "arbitrary"),
        name="norm_matmul_" + mode,
    )(*args)
    return res if has_lr else res[0]


def _attn_kernel(sink_ref, q_ref, kp_ref, kc_ref, kn_ref, kx_ref, vp_ref, vc_ref, vn_ref, vx_ref,
                 o_ref, *, nblk):
    i = pl.program_id(1)
    blk = HEAD_DIM
    dn = (((1,), (1,)), ((), ()))
    row = lax.broadcasted_iota(jnp.int32, (blk, blk), 0)
    col = lax.broadcasted_iota(jnp.int32, (blk, blk), 1)
    ninf = jnp.float32(-jnp.inf)
    bias_p = jnp.where(col >= row, 0.0, ninf) + jnp.where(i > 0, 0.0, ninf)
    bias_n = jnp.where(col <= row, 0.0, ninf) + jnp.where(i < nblk - 1, 0.0, ninf)
    bias_p = jnp.concatenate([bias_p] * Q_PER_KV, axis=0)
    bias_n = jnp.concatenate([bias_n] * Q_PER_KV, axis=0)
    n_ctx = kx_ref.shape[0] // blk

    for h in range(N_KV_HEADS):
        hs = slice(h * HEAD_DIM, (h + 1) * HEAD_DIM)
        qs = jnp.concatenate(
            [q_ref[:, (h * Q_PER_KV + g) * HEAD_DIM:(h * Q_PER_KV + g + 1) * HEAD_DIM]
             for g in range(Q_PER_KV)], axis=0)

        def scores(k_ref):
            return lax.dot_general(qs, k_ref[:, hs], dn, preferred_element_type=F32)

        s_x = scores(kx_ref)
        pieces = [scores(kp_ref) + bias_p, scores(kc_ref), scores(kn_ref) + bias_n]
        pieces += [s_x[:, j * blk:(j + 1) * blk] for j in range(n_ctx)]
        sink = jnp.concatenate(
            [jnp.full((blk, 1), sink_ref[h * Q_PER_KV + g] * LOG2E, F32) for g in range(Q_PER_KV)], axis=0)
        mx = pieces[0]
        for s in pieces[1:]:
            mx = jnp.maximum(mx, s)
        m = jnp.maximum(jnp.max(mx, axis=-1, keepdims=True), sink)
        probs = [jnp.exp2(s - m) for s in pieces]
        psum = probs[0]
        for pr in probs[1:]:
            psum = psum + pr
        denom = jnp.exp2(sink - m) + jnp.sum(psum, axis=-1, keepdims=True)
        p_x = jnp.concatenate(probs[3:], axis=1) if n_ctx > 1 else probs[3]
        o = (jnp.dot(probs[0].astype(BF16), vp_ref[:, hs], preferred_element_type=F32)
             + jnp.dot(probs[1].astype(BF16), vc_ref[:, hs], preferred_element_type=F32)
             + jnp.dot(probs[2].astype(BF16), vn_ref[:, hs], preferred_element_type=F32)
             + jnp.dot(p_x.astype(BF16), vx_ref[:, hs], preferred_element_type=F32))
        o = o / denom
        for g in range(Q_PER_KV):
            c0 = (h * Q_PER_KV + g) * HEAD_DIM
            o_ref[:, c0:c0 + HEAD_DIM] = o[g * blk:(g + 1) * blk].astype(o_ref.dtype)


def _attention(qk, p, qk_c, p_c, sink, batch, seq, ctx_len):
    blk = HEAD_DIM
    nblk = seq // blk
    kcol = ATTN_WIDTH // KV_WIDTH
    vcol = P_VA // KV_WIDTH

    def kv_spec(col0, shift):
        def imap(b, i):
            return (b * nblk + jnp.clip(i + shift, 0, nblk - 1), col0)
        return pl.BlockSpec((blk, KV_WIDTH), imap)

    in_specs = [pl.BlockSpec(memory_space=pltpu.SMEM),
                pl.BlockSpec((blk, ATTN_WIDTH), lambda b, i: (b * nblk + i, 0)),
                kv_spec(kcol, -1), kv_spec(kcol, 0), kv_spec(kcol, 1),
                pl.BlockSpec((ctx_len, KV_WIDTH), lambda b, i: (b, kcol)),
                kv_spec(vcol, -1), kv_spec(vcol, 0), kv_spec(vcol, 1),
                pl.BlockSpec((ctx_len, KV_WIDTH), lambda b, i: (b, vcol))]
    return pl.pallas_call(
        functools.partial(_attn_kernel, nblk=nblk),
        grid=(batch, nblk),
        in_specs=in_specs,
        out_specs=pl.BlockSpec((blk, ATTN_WIDTH), lambda b, i: (b * nblk + i, 0)),
        out_shape=jax.ShapeDtypeStruct((batch * seq, ATTN_WIDTH), BF16),
        compiler_params=_params("parallel", "arbitrary"),
        name="window_attention",
    )(sink, qk, qk, qk, qk, qk_c, p, p, p, p_c)


def _gla_masks():
    c = GLA_CHUNK
    t = np.arange(c)[:, None]
    s = np.arange(c)[None, :]
    fwd = []
    for w in GLA_LEVELS:
        fwd.append((t // (2 * w) == s // (2 * w)) & (t % (2 * w) >= w) & (s % (2 * w) < w))
    fwd.append((t // GLA_SUB == s // GLA_SUB) & (s <= t))
    fwd.append(s <= t)
    fwd = np.stack(fwd).astype(np.float32)
    return np.stack([fwd, np.transpose(fwd, (0, 2, 1))])


def _gla_chunk(q_ref, k_ref, v_ref, lr_ref, wg_ref, bg_ref, mask_ref, st_ref, w_ref, b_ref, o_ref, reverse, hp):
    c = GLA_CHUNK
    nl = len(GLA_LEVELS)
    ksl = slice(hp * GLA_DK, (hp + 1) * GLA_DK)
    vsl = slice(hp * GLA_DV, (hp + 1) * GLA_DV)
    q = q_ref[:, ksl].astype(F32)
    kb = k_ref[:, ksl]
    k = kb.astype(F32)
    v = v_ref[:, vsl]
    xg = jnp.dot(lr_ref[...], wg_ref[:, ksl], preferred_element_type=F32,
                 precision=lax.Precision.HIGHEST) + bg_ref[:, ksl]
    yield
    g = (jnp.minimum(xg, 0.0) - jnp.log1p(jnp.exp(-jnp.abs(xg)))) * (LOG2E / GLA_GATE_NORM)

    tri = mask_ref[nl + 1].astype(BF16)
    g_hi = g.astype(BF16)
    r1 = g - g_hi.astype(F32)
    g_mid = r1.astype(BF16)
    g_lo = (r1 - g_mid.astype(F32)).astype(BF16)
    yield
    b = (jnp.dot(tri, g_hi, preferred_element_type=F32) + jnp.dot(tri, g_mid, preferred_element_type=F32)
         + jnp.dot(tri, g_lo, preferred_element_type=F32))
    b_ref[...] = b
    yield

    dn_nt = (((1,), (1,)), ((), ()))
    a_mat = None
    for lvl, w in enumerate(GLA_LEVELS):
        zero = jnp.zeros((w, GLA_DK), F32)
        qparts, kparts = [], []
        for mblk in range(c // (2 * w)):
            lo, mid, hi = mblk * 2 * w, mblk * 2 * w + w, mblk * 2 * w + 2 * w
            if reverse:
                r = b[mid:mid + 1, :]
                qparts += [q[lo:mid] * jnp.exp2(b[lo:mid] - r), zero]
                kparts += [zero, k[mid:hi] * jnp.exp2(r - b[mid:hi])]
            else:
                r = b[mid - 1:mid, :]
                qparts += [zero, q[mid:hi] * jnp.exp2(b[mid:hi] - r)]
                kparts += [k[lo:mid] * jnp.exp2(r - b[lo:mid]), zero]
        qn = jnp.concatenate(qparts, axis=0).astype(BF16)
        kn = jnp.concatenate(kparts, axis=0).astype(BF16)
        term = lax.dot_general(qn, kn, dn_nt, preferred_element_type=F32)
        if 2 * w < c:
            term = term * mask_ref[lvl]
        a_mat = term if a_mat is None else a_mat + term
        yield

    for i in range(c // GLA_SUB):
        r0 = i * GLA_SUB
        qi = q[r0:r0 + GLA_SUB]
        bi = b[r0:r0 + GLA_SUB]
        for s in range(0, GLA_SUB, 2):
            e0 = jnp.exp2(bi - b_ref[pl.ds(r0 + s, 1), :])
            e1 = jnp.exp2(bi - b_ref[pl.ds(r0 + s + 1, 1), :])
            w_ref[pl.ds((r0 + s) * GLA_SUB, 2 * GLA_SUB), :] = jnp.concatenate(
                [qi * e0, qi * e1], axis=0).astype(BF16)
        yield
    red = lax.dot_general(w_ref[...], kb, dn_nt, preferred_element_type=F32)
    yield
    lane = lax.broadcasted_iota(jnp.int32, (GLA_SUB, c), 1)
    rows = []
    for i in range(c // GLA_SUB):
        blk = jnp.zeros((GLA_SUB, c), F32)
        for s in range(GLA_SUB):
            col = i * GLA_SUB + s
            blk = jnp.where(lane == col, red[col * GLA_SUB:(col + 1) * GLA_SUB], blk)
        rows.append(blk)
        if i % 4 == 3:
            yield
    a_mat = a_mat + jnp.where(mask_ref[nl] > 0.5, jnp.concatenate(rows, axis=0), 0.0)

    st = st_ref[...]
    qe = (q * jnp.exp2(b)).astype(BF16)
    o = lax.dot_general(qe, st.astype(BF16), dn_nt, preferred_element_type=F32)
    yield
    o = o + jnp.dot(a_mat.astype(BF16), v, preferred_element_type=F32)
    o_ref[:, vsl] = o.astype(o_ref.dtype)
    yield

    r_end = 0 if reverse else c - 1
    b_end = b[r_end:r_end + 1, :]
    ke = (k * jnp.exp2(b_end - b)).astype(BF16)
    upd = lax.dot_general(v, ke, (((0,), (0,)), ((), ())), preferred_element_type=F32)
    yield
    st_ref[...] = st * jnp.exp2(b_end) + upd


def _gla_kernel(qf_ref, kf_ref, vf_ref, lrf_ref, qb_ref, kb_ref, vb_ref, lrb_ref,
                wgf_ref, bgf_ref, wgb_ref, bgb_ref, mask_ref, s0f_ref, s0b_ref,
                of_ref, ob_ref, sf_ref, sb_ref, stf_ref, stb_ref, wf_ref, wb_ref, bf_ref, bb_ref):
    cidx = pl.program_id(2)

    @pl.when(cidx == 0)
    def _():
        stf_ref[...] = s0f_ref[0]
        stb_ref[...] = s0b_ref[0]

    chains = []
    for hp in range(GLA_HEADS_PER_STEP):
        chains.append(_gla_chunk(qf_ref, kf_ref, vf_ref, lrf_ref, wgf_ref, bgf_ref, mask_ref.at[0],
                                 stf_ref.at[hp], wf_ref.at[hp], bf_ref.at[hp], of_ref, False, hp))
        chains.append(_gla_chunk(qb_ref, kb_ref, vb_ref, lrb_ref, wgb_ref, bgb_ref, mask_ref.at[1],
                                 stb_ref.at[hp], wb_ref.at[hp], bb_ref.at[hp], ob_ref, True, hp))
    while chains:
        chains = [ch for ch in chains if next(ch, True) is None]

    @pl.when(cidx == pl.num_programs(2) - 1)
    def _():
        sf_ref[0] = stf_ref[...]
        sb_ref[0] = stb_ref[...]


def _gla(p, lr, wgf, bgf, wgb, bgb, s0f, s0b, batch, seq):
    c = GLA_CHUNK
    nc = seq // c
    hps = GLA_HEADS_PER_STEP
    kw, vw = hps * GLA_DK, hps * GLA_DV
    kcol = P_KB // kw
    vcol = P_VB // vw

    def fwd(b, h, i):
        return b * nc + i

    def bwd(b, h, i):
        return b * nc + (nc - 1 - i)

    def data_specs(rowf):
        return [pl.BlockSpec((c, kw), lambda b, h, i: (rowf(b, h, i), h)),
                pl.BlockSpec((c, kw), lambda b, h, i: (rowf(b, h, i), kcol + h)),
                pl.BlockSpec((c, vw), lambda b, h, i: (rowf(b, h, i), vcol + h)),
                pl.BlockSpec((c, LANES), lambda b, h, i: (rowf(b, h, i), 0))]

    gate_specs = [pl.BlockSpec((LANES, kw), lambda b, h, i: (0, h)),
                  pl.BlockSpec((1, kw), lambda b, h, i: (0, h))]
    state_spec = pl.BlockSpec((1, hps, GLA_DV, GLA_DK), lambda b, h, i: (b, h, 0, 0))
    masks = jnp.asarray(_gla_masks())
    in_specs = (data_specs(fwd) + data_specs(bwd) + gate_specs + gate_specs
                + [pl.BlockSpec(masks.shape, lambda b, h, i: (0, 0, 0, 0)), state_spec, state_spec])
    out_specs = [pl.BlockSpec((c, vw), lambda b, h, i: (fwd(b, h, i), h)),
                 pl.BlockSpec((c, vw), lambda b, h, i: (bwd(b, h, i), h)),
                 state_spec, state_spec]
    out_shape = [jax.ShapeDtypeStruct((batch * seq, GLA_V_WIDTH), BF16)] * 2 + [
        jax.ShapeDtypeStruct((batch, GLA_HEADS, GLA_DV, GLA_DK), F32)] * 2
    return pl.pallas_call(
        _gla_kernel,
        grid=(batch, GLA_HEADS // hps, nc),
        in_specs=in_specs,
        out_specs=out_specs,
        out_shape=out_shape,
        scratch_shapes=[pltpu.VMEM((hps, GLA_DV, GLA_DK), F32), pltpu.VMEM((hps, GLA_DV, GLA_DK), F32),
                        pltpu.VMEM((hps, c * GLA_SUB, GLA_DK), BF16),
                        pltpu.VMEM((hps, c * GLA_SUB, GLA_DK), BF16),
                        pltpu.VMEM((hps, c, GLA_DK), F32), pltpu.VMEM((hps, c, GLA_DK), F32)],
        compiler_params=_params("parallel", "parallel", "arbitrary"),
        name="gla_scan",
    )(p, p, p, lr, p, p, p, lr, wgf, bgf, wgb, bgb, masks, s0f, s0b)


def _merge_kernel(oa_ref, of_ref, ob_ref, rb_ref, ga_ref, gb_ref, gn_ref, wa_ref, wg_ref, z_ref):
    og = of_ref[...].astype(F32) + ob_ref[...].astype(F32)
    parts = []
    for hh in range(GLA_HEADS):
        a = og[:, hh * GLA_DV:(hh + 1) * GLA_DV]
        ms = jnp.mean(a * a, axis=-1, keepdims=True)
        parts.append(a * lax.rsqrt(ms + EPS) * gn_ref[...])
    n = jnp.concatenate(parts, axis=1)
    rb = rb_ref[...].astype(F32)
    n = (n * (rb * _sigmoid(rb))).astype(BF16)
    y_gla = jnp.dot(n, wg_ref[...], preferred_element_type=F32)
    y_att = jnp.dot(oa_ref[...], wa_ref[...], preferred_element_type=F32)
    z = _sigmoid(ga_ref[...].astype(F32)) * y_att + _sigmoid(gb_ref[...].astype(F32)) * y_gla
    z_ref[...] = z.astype(z_ref.dtype)


def _merge(o_attn, o_f, o_b, p, gla_norm, w_attn_o, w_gla_o, tm=256):
    m, d = o_attn.shape
    row = lambda i: (i, 0)
    const = lambda i: (0, 0)
    wspec = pl.BlockSpec((d, d), const, pipeline_mode=pl.Buffered(1))
    return pl.pallas_call(
        _merge_kernel,
        grid=(m // tm,),
        in_specs=[pl.BlockSpec((tm, d), row), pl.BlockSpec((tm, d), row), pl.BlockSpec((tm, d), row),
                  pl.BlockSpec((tm, d), lambda i: (i, P_RB // d)),
                  pl.BlockSpec((tm, d), lambda i: (i, P_GA // d)),
                  pl.BlockSpec((tm, d), lambda i: (i, P_GB // d)),
                  pl.BlockSpec((1, GLA_DV), const), wspec, wspec],
        out_specs=pl.BlockSpec((tm, d), row),
        out_shape=jax.ShapeDtypeStruct((m, d), BF16),
        compiler_params=_params("parallel"),
        name="merge_gates",
    )(o_attn, o_f, o_b, p, p, p, gla_norm.reshape(1, GLA_DV), w_attn_o, w_gla_o)


def _out_proj_kernel(z_ref, w_ref, x_ref, gate_ref, o_ref):
    y = jnp.dot(z_ref[...], w_ref[...], preferred_element_type=F32)
    o_ref[...] = x_ref[...] + gate_ref[0] * y


def _out_proj(z, w_out, x2, gate, rows_per_batch, tm=512):
    m, d = x2.shape
    tm = min(tm, rows_per_batch)
    tpb = rows_per_batch // tm
    return pl.pallas_call(
        _out_proj_kernel,
        grid=(m // tm,),
        in_specs=[pl.BlockSpec((tm, d), lambda i: (i, 0)),
                  pl.BlockSpec((d, d), lambda i: (0, 0), pipeline_mode=pl.Buffered(1)),
                  pl.BlockSpec((tm, d), lambda i: (i, 0)),
                  pl.BlockSpec((1, 1, d), lambda i: (i // tpb, 0, 0))],
        out_specs=pl.BlockSpec((tm, d), lambda i: (i, 0)),
        out_shape=jax.ShapeDtypeStruct((m, d), F32),
        compiler_params=_params("parallel"),
        name="out_proj_residual",
    )(z, w_out, x2, gate)


def _ffn_down_kernel(ua_ref, uap_ref, uan_ref, ug_ref, ugp_ref, ugn_ref, cwa_ref, cwg_ref, cba_ref,
                     cbg_ref, wd_ref, x_ref, gate_ref, o_ref, acc_ref, *, tpb, halo):
    i = pl.program_id(0)
    kk = pl.program_id(1)
    tm = ua_ref.shape[0]
    keep_prev = jnp.where((i % tpb) == 0, 0.0, 1.0)
    keep_next = jnp.where((i % tpb) == tpb - 1, 0.0, 1.0)
    rowi = lax.broadcasted_iota(jnp.int32, ua_ref.shape, 0)

    def conv(u_ref, up_ref, un_ref, cw_ref, cb_ref):
        u = u_ref[...].astype(F32)
        prev_row = up_ref[halo - 1:halo, :].astype(F32) * keep_prev
        next_row = un_ref[0:1, :].astype(F32) * keep_next
        u_prev = jnp.where(rowi == 0, prev_row, pltpu.roll(u, 1, axis=0))
        u_next = jnp.where(rowi == tm - 1, next_row, pltpu.roll(u, tm - 1, axis=0))
        return cw_ref[0:1, :] * u_prev + cw_ref[1:2, :] * u + cw_ref[2:3, :] * u_next + cb_ref[...]

    a = conv(ua_ref, uap_ref, uan_ref, cwa_ref, cba_ref)
    g = conv(ug_ref, ugp_ref, ugn_ref, cwg_ref, cbg_ref)
    act = (a * _sigmoid(a) * g).astype(BF16)
    contrib = jnp.dot(act, wd_ref[...], preferred_element_type=F32)

    @pl.when(kk == 0)
    def _():
        acc_ref[...] = contrib

    @pl.when(kk > 0)
    def _():
        acc_ref[...] += contrib

    @pl.when(kk == pl.num_programs(1) - 1)
    def _():
        o_ref[...] = x_ref[...] + gate_ref[0] * acc_ref[...]


def _ffn_down(u, conv_w, conv_b, w_down, x2, gate, rows_per_batch, tm=512, tk=512):
    m, d = x2.shape
    dff = w_down.shape[0]
    tm = min(tm, rows_per_batch)
    tpb = rows_per_batch // tm
    halo = 16
    hb = tm // halo
    nk = dff // tk
    nhalo = m // halo

    def main(off):
        return pl.BlockSpec((tm, tk), lambda i, k: (i, off + k))

    def prev(off):
        return pl.BlockSpec((halo, tk), lambda i, k: (jnp.maximum(i * hb - 1, 0), off + k))

    def nxt(off):
        return pl.BlockSpec((halo, tk), lambda i, k: (jnp.minimum((i + 1) * hb, nhalo - 1), off + k))

    def vec(rows, off):
        return pl.BlockSpec((rows, tk), lambda i, k: (0, off + k))

    return pl.pallas_call(
        functools.partial(_ffn_down_kernel, tpb=tpb, halo=halo),
        grid=(m // tm, nk),
        in_specs=[main(0), prev(0), nxt(0), main(nk), prev(nk), nxt(nk),
                  vec(3, 0), vec(3, nk), vec(1, 0), vec(1, nk),
                  pl.BlockSpec((tk, d), lambda i, k: (k, 0)),
                  pl.BlockSpec((tm, d), lambda i, k: (i, 0)),
                  pl.BlockSpec((1, 1, d), lambda i, k: (i // tpb, 0, 0))],
        out_specs=pl.BlockSpec((tm, d), lambda i, k: (i, 0)),
        out_shape=jax.ShapeDtypeStruct((m, d), F32),
        scratch_shapes=[pltpu.VMEM((tm, d), F32)],
        compiler_params=_params("parallel", "arbitrary"),
        name="convffn_down",
    )(u, u, u, u, u, u, conv_w, conv_w, conv_b.reshape(1, 2 * dff), conv_b.reshape(1, 2 * dff),
      w_down, x2, gate)


def _rope_tables(n):
    rows = n // GRID_W
    row = jnp.repeat(jnp.arange(rows), GRID_W)
    col = jnp.tile(jnp.arange(GRID_W), rows)
    n_freq = HEAD_DIM // 4
    inv = ROPE_THETA ** (-jnp.arange(n_freq, dtype=F32) / n_freq)
    ang = jnp.concatenate([row[:, None] * inv, col[:, None] * inv], axis=-1)
    cos, sin = jnp.cos(ang), jnp.sin(ang)
    return jnp.concatenate([cos, cos], axis=-1), jnp.concatenate([-sin, sin], axis=-1)


def _split_w_in(w_in):
    splits = (ATTN_WIDTH, KV_WIDTH, KV_WIDTH, GLA_K_WIDTH, GLA_K_WIDTH, GLA_V_WIDTH, GLA_V_WIDTH,
              GLA_LOWRANK, GLA_LOWRANK, D_MODEL, D_MODEL)
    offs = np.cumsum(splits)[:-1].tolist()
    qa, ka, va, qb, kb, vb, rb, lrf, lrb, ga, gb = jnp.split(w_in, offs, axis=-1)
    w_qk = jnp.concatenate([qa, ka], axis=-1).astype(BF16)
    w_p = jnp.concatenate([qb, kb, vb, rb, ga, gb, va], axis=-1).astype(BF16)
    pad = jnp.zeros((w_in.shape[0], LANES - 2 * GLA_LOWRANK), w_in.dtype)
    w_lr = jnp.concatenate([lrf, lrb, pad], axis=-1).astype(BF16)
    return w_qk, w_p, w_lr


def kernel(x, c, ctx, c_ctx, w_mod, b_mod, g_mix, w_in, q_norm, k_norm, attn_sink, w_gate_f, b_gate_f,
           w_gate_b, b_gate_b, gla_norm, w_attn_o, w_gla_o, w_out, g_ffn, w_up, conv_w, conv_b, w_down):
    batch, seq, d = x.shape
    ctx_len = ctx.shape[1]
    assert w_mod.shape[0] == 1, "single-layer kernel"
    assert d == D_MODEL and seq % GLA_CHUNK == 0 and ctx_len % GLA_CHUNK == 0 and batch <= 7

    cc = jnp.zeros((8, d), F32).at[:batch].set(c).at[batch].set(c_ctx)
    mod = _modulation(cc, w_mod[0], b_mod[0]).reshape(8, 6, d)
    mod_x = [mod[:batch, jj][:, None, :] for jj in range(6)]
    mod_c = [mod[batch:batch + 1, jj][:, None, :] for jj in range(6)]

    w_qk, w_p, w_lr = _split_w_in(w_in[0])
    q_fold = HEAD_DIM ** -0.5 * LOG2E
    qk_norm_w = jnp.concatenate([jnp.tile(q_norm[0] * q_fold, N_Q_HEADS), jnp.tile(k_norm[0], N_KV_HEADS)])
    p_scale = jnp.ones((P_WIDTH,), F32).at[P_QB:P_QB + GLA_K_WIDTH].set(GLA_DK ** -0.5)
    cos2, sin2 = _rope_tables(seq)

    x2 = x.reshape(batch * seq, d)
    c2 = ctx.reshape(batch * ctx_len, d)
    qk, lr = _norm_matmul(x2, mod_x[0], mod_x[1], g_mix[0], w_qk, qk_norm_w, mode="qk",
                          rows_per_batch=seq, cos2=cos2, sin2=sin2, w_lr=w_lr, tm=512, tn=QK_WIDTH)
    p = _norm_matmul(x2, mod_x[0], mod_x[1], g_mix[0], w_p, p_scale, mode="plain", rows_per_batch=seq,
                     tn=1536)
    qk_c, lr_c = _norm_matmul(c2, mod_c[0], mod_c[1], g_mix[0], w_qk, qk_norm_w, mode="qk",
                              rows_per_batch=batch * ctx_len, w_lr=w_lr, tm=512, tn=QK_WIDTH)
    p_c = _norm_matmul(c2, mod_c[0], mod_c[1], g_mix[0], w_p, p_scale, mode="plain",
                       rows_per_batch=batch * ctx_len)

    o_attn = _attention(qk, p, qk_c, p_c, attn_sink[0], batch, seq, ctx_len)

    wgf = jnp.zeros((LANES, GLA_K_WIDTH), F32).at[:GLA_LOWRANK].set(w_gate_f[0])
    wgb = jnp.zeros((LANES, GLA_K_WIDTH), F32).at[GLA_LOWRANK:2 * GLA_LOWRANK].set(w_gate_b[0])
    bgf = b_gate_f[0].reshape(1, GLA_K_WIDTH)
    bgb = b_gate_b[0].reshape(1, GLA_K_WIDTH)
    zero_state = jnp.zeros((batch, GLA_HEADS, GLA_DV, GLA_DK), F32)
    _, _, sf, sb = _gla(p_c, lr_c, wgf, bgf, wgb, bgb, zero_state, zero_state, batch, ctx_len)
    o_f, o_b, _, _ = _gla(p, lr, wgf, bgf, wgb, bgb, sf, sb, batch, seq)

    z = _merge(o_attn, o_f, o_b, p, gla_norm[0], w_attn_o[0].astype(BF16), w_gla_o[0].astype(BF16))
    x1 = _out_proj(z, w_out[0].astype(BF16), x2, mod_x[2], seq)

    ones_ff = jnp.ones((2 * D_FF,), F32)
    u = _norm_matmul(x1, mod_x[3], mod_x[4], g_ffn[0], w_up[0].astype(BF16), ones_ff, mode="plain",
                     rows_per_batch=seq, tn=1024)
    out = _ffn_down(u, conv_w[0], conv_b[0], w_down[0].astype(BF16), x1, mod_x[5], seq)
    return out.reshape(batch, seq, d)
```

```python
import functools

import numpy as np
import jax
import jax.numpy as jnp
from jax import lax
from jax.experimental import pallas as pl
from jax.experimental.pallas import tpu as pltpu

F32 = jnp.float32
BF16 = jnp.bfloat16

D_MODEL = 2048
GRID_W = 64
HEAD_DIM = 128
N_Q_HEADS = 16
N_KV_HEADS = 4
Q_PER_KV = N_Q_HEADS // N_KV_HEADS
WINDOW = 128
ROPE_THETA = 10000.0
GLA_HEADS = 4
GLA_DK = D_MODEL // 2 // GLA_HEADS
GLA_DV = D_MODEL // GLA_HEADS
GLA_LOWRANK = 16
GLA_GATE_NORM = 16.0
D_FF = 5632
EPS = 1e-6
ATTN_WIDTH = N_Q_HEADS * HEAD_DIM
KV_WIDTH = N_KV_HEADS * HEAD_DIM
GLA_K_WIDTH = GLA_HEADS * GLA_DK
GLA_V_WIDTH = GLA_HEADS * GLA_DV

LANES = 128
MXU_WIDTH = 256
VMEM_LIMIT_BYTES = 56 * 1024 * 1024

GLA_CHUNK = 128
GLA_SUB = 8
GLA_LEVELS = (64, 32, 16, 8)
LOG2E = 1.4426950408889634
GLA_HEADS_PER_STEP = 4

P_QB, P_KB, P_VB, P_RB, P_GA, P_GB, P_VA = 0, 1024, 2048, 4096, 6144, 8192, 10240
P_WIDTH = 10752
QK_WIDTH = ATTN_WIDTH + KV_WIDTH
QK_GROUP = 512


def _params(*sem):
    return pltpu.CompilerParams(dimension_semantics=sem, vmem_limit_bytes=VMEM_LIMIT_BYTES)


def _sigmoid(x):
    return 1.0 / (1.0 + jnp.exp(-x))


def _mod_kernel(a_ref, w_ref, b_ref, o_ref):
    a = a_ref[...]
    s = (a * _sigmoid(a)).astype(BF16)
    o_ref[...] = jnp.dot(s, w_ref[...].astype(BF16), preferred_element_type=F32) + b_ref[...]


def _modulation(cc, w_mod, b_mod):
    d, n = w_mod.shape
    tn = 1024
    return pl.pallas_call(
        _mod_kernel,
        grid=(n // tn,),
        in_specs=[pl.BlockSpec((8, d), lambda j: (0, 0)),
                  pl.BlockSpec((d, tn), lambda j: (0, j)),
                  pl.BlockSpec((1, tn), lambda j: (0, j))],
        out_specs=pl.BlockSpec((8, tn), lambda j: (0, j)),
        out_shape=jax.ShapeDtypeStruct((8, n), F32),
        compiler_params=_params("parallel"),
        name="modulation",
    )(cc, w_mod, b_mod.reshape(1, n))


def _norm_mm_kernel(*refs, mode, rope, has_lr, tn):
    it = iter(refs)
    x_ref, shift_ref, scale_ref, gain_ref, w_ref, cs_ref = [next(it) for _ in range(6)]
    cos_ref = sin_ref = wlr_ref = lr_ref = seg_ref = perm_ref = None
    if mode == "qk":
        seg_ref, perm_ref = next(it), next(it)
    if rope:
        cos_ref, sin_ref = next(it), next(it)
    if has_lr:
        wlr_ref = next(it)
    o_ref = next(it)
    if has_lr:
        lr_ref = next(it)
    h_ref = next(it)

    @pl.when(pl.program_id(1) == 0)
    def _():
        x = x_ref[...]
        ms = jnp.mean(x * x, axis=-1, keepdims=True)
        y = x * lax.rsqrt(ms + EPS) * gain_ref[...]
        h = (y * (1.0 + scale_ref[0]) + shift_ref[0]).astype(BF16)
        h_ref[...] = h
        if has_lr:
            lr_ref[...] = jnp.dot(h, wlr_ref[...], preferred_element_type=F32)

    if mode == "plain":
        acc = jnp.dot(h_ref[...], w_ref[...], preferred_element_type=F32)
        o_ref[...] = (acc * cs_ref[...]).astype(o_ref.dtype)
    else:
        for gb in range(tn // QK_GROUP):
            big = jnp.dot(h_ref[...], w_ref[:, gb * QK_GROUP:(gb + 1) * QK_GROUP],
                          preferred_element_type=F32)
            for cb in range(QK_GROUP // MXU_WIDTH):
                c0 = gb * QK_GROUP + cb * MXU_WIDTH
                acc = big[:, cb * MXU_WIDTH:(cb + 1) * MXU_WIDTH]
                ss = jnp.dot((acc * acc).astype(BF16), seg_ref[...], preferred_element_type=F32)
                an = acc * lax.rsqrt(ss * (1.0 / HEAD_DIM) + EPS) * cs_ref[:, c0:c0 + MXU_WIDTH]
                if rope:
                    rot = jnp.dot(an.astype(BF16), perm_ref[...], preferred_element_type=F32)
                for hh in range(MXU_WIDTH // HEAD_DIM):
                    hs = slice(hh * HEAD_DIM, (hh + 1) * HEAD_DIM)
                    a = an[:, hs]
                    if rope:
                        a = a * cos_ref[...] + rot[:, hs] * sin_ref[...]
                    o_ref[:, c0 + hh * HEAD_DIM:c0 + (hh + 1) * HEAD_DIM] = a.astype(o_ref.dtype)


def _norm_matmul(x2, shift, scale, gain, w, colvec, *, mode, rows_per_batch, cos2=None, sin2=None,
                 w_lr=None, tm=1024, tn=512):
    m, d = x2.shape
    n = w.shape[1]
    tm = min(tm, rows_per_batch)
    tpb = rows_per_batch // tm
    rope = cos2 is not None
    has_lr = w_lr is not None
    in_specs = [pl.BlockSpec((tm, d), lambda i, j: (i, 0)),
                pl.BlockSpec((1, 1, d), lambda i, j: (i // tpb, 0, 0)),
                pl.BlockSpec((1, 1, d), lambda i, j: (i // tpb, 0, 0)),
                pl.BlockSpec((1, d), lambda i, j: (0, 0)),
                pl.BlockSpec((d, tn), lambda i, j: (0, j)),
                pl.BlockSpec((1, tn), lambda i, j: (0, j))]
    args = [x2, shift, scale, gain.reshape(1, d), w, colvec.reshape(1, n)]
    if mode == "qk":
        lane = np.arange(MXU_WIDTH)
        same_head = lane[:, None] // HEAD_DIM == lane[None, :] // HEAD_DIM
        rolled = (lane[:, None] % HEAD_DIM) == ((lane[None, :] - HEAD_DIM // 2) % HEAD_DIM)
        in_specs += [pl.BlockSpec((MXU_WIDTH, MXU_WIDTH), lambda i, j: (0, 0))] * 2
        args += [jnp.asarray(same_head, BF16), jnp.asarray(same_head & rolled, BF16)]
    if rope:
        in_specs += [pl.BlockSpec((tm, HEAD_DIM), lambda i, j: (i % tpb, 0))] * 2
        args += [cos2, sin2]
    out_shape = [jax.ShapeDtypeStruct((m, n), BF16)]
    out_specs = [pl.BlockSpec((tm, tn), lambda i, j: (i, j))]
    if has_lr:
        in_specs.append(pl.BlockSpec((d, LANES), lambda i, j: (0, 0)))
        args.append(w_lr)
        out_shape.append(jax.ShapeDtypeStruct((m, LANES), F32))
        out_specs.append(pl.BlockSpec((tm, LANES), lambda i, j: (i, 0)))
    res = pl.pallas_call(
        functools.partial(_norm_mm_kernel, mode=mode, rope=rope, has_lr=has_lr, tn=tn),
        grid=(m // tm, n // tn),
        in_specs=in_specs,
        out_specs=out_specs,
        out_shape=out_shape,
        scratch_shapes=[pltpu.VMEM((tm, d), BF16)],
        compiler_params=_params("parallel", "arbitrary"),
        name="norm_matmul_" + mode,
    )(*args)
    return res if has_lr else res[0]


def _attn_kernel(sink_ref, q_ref, kp_ref, kc_ref, kn_ref, kx_ref, vp_ref, vc_ref, vn_ref, vx_ref,
                 o_ref, *, nblk):
    i = pl.program_id(1)
    blk = HEAD_DIM
    dn = (((1,), (1,)), ((), ()))
    row = lax.broadcasted_iota(jnp.int32, (blk, blk), 0)
    col = lax.broadcasted_iota(jnp.int32, (blk, blk), 1)
    ninf = jnp.float32(-jnp.inf)
    bias_p = jnp.where(col >= row, 0.0, ninf) + jnp.where(i > 0, 0.0, ninf)
    bias_n = jnp.where(col <= row, 0.0, ninf) + jnp.where(i < nblk - 1, 0.0, ninf)
    bias_p = jnp.concatenate([bias_p] * Q_PER_KV, axis=0)
    bias_n = jnp.concatenate([bias_n] * Q_PER_KV, axis=0)
    n_ctx = kx_ref.shape[0] // blk

    for h in range(N_KV_HEADS):
        hs = slice(h * HEAD_DIM, (h + 1) * HEAD_DIM)
        qs = jnp.concatenate(
            [q_ref[:, (h * Q_PER_KV + g) * HEAD_DIM:(h * Q_PER_KV + g + 1) * HEAD_DIM]
             for g in range(Q_PER_KV)], axis=0)

        def scores(k_ref):
            return lax.dot_general(qs, k_ref[:, hs], dn, preferred_element_type=F32)

        s_x = scores(kx_ref)
        pieces = [scores(kp_ref) + bias_p, scores(kc_ref), scores(kn_ref) + bias_n]
        pieces += [s_x[:, j * blk:(j + 1) * blk] for j in range(n_ctx)]
        sink = jnp.concatenate(
            [jnp.full((blk, 1), sink_ref[h * Q_PER_KV + g] * LOG2E, F32) for g in range(Q_PER_KV)], axis=0)
        mx = pieces[0]
        for s in pieces[1:]:
            mx = jnp.maximum(mx, s)
        m = jnp.maximum(jnp.max(mx, axis=-1, keepdims=True), sink)
        probs = [jnp.exp2(s - m) for s in pieces]
        psum = probs[0]
        for pr in probs[1:]:
            psum = psum + pr
        denom = jnp.exp2(sink - m) + jnp.sum(psum, axis=-1, keepdims=True)
        p_x = jnp.concatenate(probs[3:], axis=1) if n_ctx > 1 else probs[3]
        o = (jnp.dot(probs[0].astype(BF16), vp_ref[:, hs], preferred_element_type=F32)
             + jnp.dot(probs[1].astype(BF16), vc_ref[:, hs], preferred_element_type=F32)
             + jnp.dot(probs[2].astype(BF16), vn_ref[:, hs], preferred_element_type=F32)
             + jnp.dot(p_x.astype(BF16), vx_ref[:, hs], preferred_element_type=F32))
        o = o / denom
        for g in range(Q_PER_KV):
            c0 = (h * Q_PER_KV + g) * HEAD_DIM
            o_ref[:, c0:c0 + HEAD_DIM] = o[g * blk:(g + 1) * blk].astype(o_ref.dtype)


def _attention(qk, p, qk_c, p_c, sink, batch, seq, ctx_len):
    blk = HEAD_DIM
    nblk = seq // blk
    kcol = ATTN_WIDTH // KV_WIDTH
    vcol = P_VA // KV_WIDTH

    def kv_spec(col0, shift):
        def imap(b, i):
            return (b * nblk + jnp.clip(i + shift, 0, nblk - 1), col0)
        return pl.BlockSpec((blk, KV_WIDTH), imap)

    in_specs = [pl.BlockSpec(memory_space=pltpu.SMEM),
                pl.BlockSpec((blk, ATTN_WIDTH), lambda b, i: (b * nblk + i, 0)),
                kv_spec(kcol, -1), kv_spec(kcol, 0), kv_spec(kcol, 1),
                pl.BlockSpec((ctx_len, KV_WIDTH), lambda b, i: (b, kcol)),
                kv_spec(vcol, -1), kv_spec(vcol, 0), kv_spec(vcol, 1),
                pl.BlockSpec((ctx_len, KV_WIDTH), lambda b, i: (b, vcol))]
    return pl.pallas_call(
        functools.partial(_attn_kernel, nblk=nblk),
        grid=(batch, nblk),
        in_specs=in_specs,
        out_specs=pl.BlockSpec((blk, ATTN_WIDTH), lambda b, i: (b * nblk + i, 0)),
        out_shape=jax.ShapeDtypeStruct((batch * seq, ATTN_WIDTH), BF16),
        compiler_params=_params("parallel", "arbitrary"),
        name="window_attention",
    )(sink, qk, qk, qk, qk, qk_c, p, p, p, p_c)


def _gla_masks():
    c = GLA_CHUNK
    t = np.arange(c)[:, None]
    s = np.arange(c)[None, :]
    fwd = []
    for w in GLA_LEVELS:
        fwd.append((t // (2 * w) == s // (2 * w)) & (t % (2 * w) >= w) & (s % (2 * w) < w))
    fwd.append((t // GLA_SUB == s // GLA_SUB) & (s <= t))
    fwd.append(s <= t)
    fwd = np.stack(fwd).astype(np.float32)
    return np.stack([fwd, np.transpose(fwd, (0, 2, 1))])


def _gla_chunk(q_ref, k_ref, v_ref, lr_ref, wg_ref, bg_ref, mask_ref, st_ref, w_ref, b_ref, o_ref, reverse, hp):
    c = GLA_CHUNK
    nl = len(GLA_LEVELS)
    ksl = slice(hp * GLA_DK, (hp + 1) * GLA_DK)
    vsl = slice(hp * GLA_DV, (hp + 1) * GLA_DV)
    q = q_ref[:, ksl].astype(F32)
    kb = k_ref[:, ksl]
    k = kb.astype(F32)
    v = v_ref[:, vsl]
    xg = jnp.dot(lr_ref[...], wg_ref[:, ksl], preferred_element_type=F32,
                 precision=lax.Precision.HIGHEST) + bg_ref[:, ksl]
    yield
    g = (jnp.minimum(xg, 0.0) - jnp.log1p(jnp.exp(-jnp.abs(xg)))) * (LOG2E / GLA_GATE_NORM)

    tri = mask_ref[nl + 1].astype(BF16)
    g_hi = g.astype(BF16)
    r1 = g - g_hi.astype(F32)
    g_mid = r1.astype(BF16)
    g_lo = (r1 - g_mid.astype(F32)).astype(BF16)
    yield
    b = (jnp.dot(tri, g_hi, preferred_element_type=F32) + jnp.dot(tri, g_mid, preferred_element_type=F32)
         + jnp.dot(tri, g_lo, preferred_element_type=F32))
    b_ref[...] = b
    yield

    dn_nt = (((1,), (1,)), ((), ()))
    a_mat = None
    for lvl, w in enumerate(GLA_LEVELS):
        zero = jnp.zeros((w, GLA_DK), F32)
        qparts, kparts = [], []
        for mblk in range(c // (2 * w)):
            lo, mid, hi = mblk * 2 * w, mblk * 2 * w + w, mblk * 2 * w + 2 * w
            if reverse:
                r = b[mid:mid + 1, :]
                qparts += [q[lo:mid] * jnp.exp2(b[lo:mid] - r), zero]
                kparts += [zero, k[mid:hi] * jnp.exp2(r - b[mid:hi])]
            else:
                r = b[mid - 1:mid, :]
                qparts += [zero, q[mid:hi] * jnp.exp2(b[mid:hi] - r)]
                kparts += [k[lo:mid] * jnp.exp2(r - b[lo:mid]), zero]
        qn = jnp.concatenate(qparts, axis=0).astype(BF16)
        kn = jnp.concatenate(kparts, axis=0).astype(BF16)
        term = lax.dot_general(qn, kn, dn_nt, preferred_element_type=F32)
        if 2 * w < c:
            term = term * mask_ref[lvl]
        a_mat = term if a_mat is None else a_mat + term
        yield

    for i in range(c // GLA_SUB):
        r0 = i * GLA_SUB
        qi = q[r0:r0 + GLA_SUB]
        bi = b[r0:r0 + GLA_SUB]
        for s in range(0, GLA_SUB, 2):
            e0 = jnp.exp2(bi - b_ref[pl.ds(r0 + s, 1), :])
            e1 = jnp.exp2(bi - b_ref[pl.ds(r0 + s + 1, 1), :])
            w_ref[pl.ds((r0 + s) * GLA_SUB, 2 * GLA_SUB), :] = jnp.concatenate(
                [qi * e0, qi * e1], axis=0).astype(BF16)
        yield
    red = lax.dot_general(w_ref[...], kb, dn_nt, preferred_element_type=F32)
    yield
    lane = lax.broadcasted_iota(jnp.int32, (GLA_SUB, c), 1)
    rows = []
    for i in range(c // GLA_SUB):
        blk = jnp.zeros((GLA_SUB, c), F32)
        for s in range(GLA_SUB):
            col = i * GLA_SUB + s
            blk = jnp.where(lane == col, red[col * GLA_SUB:(col + 1) * GLA_SUB], blk)
        rows.append(blk)
        if i % 4 == 3:
            yield
    a_mat = a_mat + jnp.where(mask_ref[nl] > 0.5, jnp.concatenate(rows, axis=0), 0.0)

    st = st_ref[...]
    qe = (q * jnp.exp2(b)).astype(BF16)
    o = lax.dot_general(qe, st.astype(BF16), dn_nt, preferred_element_type=F32)
    yield
    o = o + jnp.dot(a_mat.astype(BF16), v, preferred_element_type=F32)
    o_ref[:, vsl] = o.astype(o_ref.dtype)
    yield

    r_end = 0 if reverse else c - 1
    b_end = b[r_end:r_end + 1, :]
    ke = (k * jnp.exp2(b_end - b)).astype(BF16)
    upd = lax.dot_general(v, ke, (((0,), (0,)), ((), ())), preferred_element_type=F32)
    yield
    st_ref[...] = st * jnp.exp2(b_end) + upd


def _gla_kernel(qf_ref, kf_ref, vf_ref, lrf_ref, qb_ref, kb_ref, vb_ref, lrb_ref,
                wgf_ref, bgf_ref, wgb_ref, bgb_ref, mask_ref, s0f_ref, s0b_ref,
                of_ref, ob_ref, sf_ref, sb_ref, stf_ref, stb_ref, wf_ref, wb_ref, bf_ref, bb_ref):
    cidx = pl.program_id(2)

    @pl.when(cidx == 0)
    def _():
        stf_ref[...] = s0f_ref[0]
        stb_ref[...] = s0b_ref[0]

    chains = []
    for hp in range(GLA_HEADS_PER_STEP):
        chains.append(_gla_chunk(qf_ref, kf_ref, vf_ref, lrf_ref, wgf_ref, bgf_ref, mask_ref.at[0],
                                 stf_ref.at[hp], wf_ref.at[hp], bf_ref.at[hp], of_ref, False, hp))
        chains.append(_gla_chunk(qb_ref, kb_ref, vb_ref, lrb_ref, wgb_ref, bgb_ref, mask_ref.at[1],
                                 stb_ref.at[hp], wb_ref.at[hp], bb_ref.at[hp], ob_ref, True, hp))
    while chains:
        chains = [ch for ch in chains if next(ch, True) is None]

    @pl.when(cidx == pl.num_programs(2) - 1)
    def _():
        sf_ref[0] = stf_ref[...]
        sb_ref[0] = stb_ref[...]


def _gla(p, lr, wgf, bgf, wgb, bgb, s0f, s0b, batch, seq):
    c = GLA_CHUNK
    nc = seq // c
    hps = GLA_HEADS_PER_STEP
    kw, vw = hps * GLA_DK, hps * GLA_DV
    kcol = P_KB // kw
    vcol = P_VB // vw

    def fwd(b, h, i):
        return b * nc + i

    def bwd(b, h, i):
        return b * nc + (nc - 1 - i)

    def data_specs(rowf):
        return [pl.BlockSpec((c, kw), lambda b, h, i: (rowf(b, h, i), h)),
                pl.BlockSpec((c, kw), lambda b, h, i: (rowf(b, h, i), kcol + h)),
                pl.BlockSpec((c, vw), lambda b, h, i: (rowf(b, h, i), vcol + h)),
                pl.BlockSpec((c, LANES), lambda b, h, i: (rowf(b, h, i), 0))]

    gate_specs = [pl.BlockSpec((LANES, kw), lambda b, h, i: (0, h)),
                  pl.BlockSpec((1, kw), lambda b, h, i: (0, h))]
    state_spec = pl.BlockSpec((1, hps, GLA_DV, GLA_DK), lambda b, h, i: (b, h, 0, 0))
    masks = jnp.asarray(_gla_masks())
    in_specs = (data_specs(fwd) + data_specs(bwd) + gate_specs + gate_specs
                + [pl.BlockSpec(masks.shape, lambda b, h, i: (0, 0, 0, 0)), state_spec, state_spec])
    out_specs = [pl.BlockSpec((c, vw), lambda b, h, i: (fwd(b, h, i), h)),
                 pl.BlockSpec((c, vw), lambda b, h, i: (bwd(b, h, i), h)),
                 state_spec, state_spec]
    out_shape = [jax.ShapeDtypeStruct((batch * seq, GLA_V_WIDTH), BF16)] * 2 + [
        jax.ShapeDtypeStruct((batch, GLA_HEADS, GLA_DV, GLA_DK), F32)] * 2
    return pl.pallas_call(
        _gla_kernel,
        grid=(batch, GLA_HEADS // hps, nc),
        in_specs=in_specs,
        out_specs=out_specs,
        out_shape=out_shape,
        scratch_shapes=[pltpu.VMEM((hps, GLA_DV, GLA_DK), F32), pltpu.VMEM((hps, GLA_DV, GLA_DK), F32),
                        pltpu.VMEM((hps, c * GLA_SUB, GLA_DK), BF16),
                        pltpu.VMEM((hps, c * GLA_SUB, GLA_DK), BF16),
                        pltpu.VMEM((hps, c, GLA_DK), F32), pltpu.VMEM((hps, c, GLA_DK), F32)],
        compiler_params=_params("parallel", "parallel", "arbitrary"),
        name="gla_scan",
    )(p, p, p, lr, p, p, p, lr, wgf, bgf, wgb, bgb, masks, s0f, s0b)


def _merge_kernel(oa_ref, of_ref, ob_ref, rb_ref, ga_ref, gb_ref, gn_ref, wa_ref, wg_ref, z_ref):
    og = of_ref[...].astype(F32) + ob_ref[...].astype(F32)
    parts = []
    for hh in range(GLA_HEADS):
        a = og[:, hh * GLA_DV:(hh + 1) * GLA_DV]
        ms = jnp.mean(a * a, axis=-1, keepdims=True)
        parts.append(a * lax.rsqrt(ms + EPS) * gn_ref[...])
    n = jnp.concatenate(parts, axis=1)
    rb = rb_ref[...].astype(F32)
    n = (n * (rb * _sigmoid(rb))).astype(BF16)
    y_gla = jnp.dot(n, wg_ref[...], preferred_element_type=F32)
    y_att = jnp.dot(oa_ref[...], wa_ref[...], preferred_element_type=F32)
    z = _sigmoid(ga_ref[...].astype(F32)) * y_att + _sigmoid(gb_ref[...].astype(F32)) * y_gla
    z_ref[...] = z.astype(z_ref.dtype)


def _merge(o_attn, o_f, o_b, p, gla_norm, w_attn_o, w_gla_o, tm=256):
    m, d = o_attn.shape
    row = lambda i: (i, 0)
    const = lambda i: (0, 0)
    wspec = pl.BlockSpec((d, d), const, pipeline_mode=pl.Buffered(1))
    return pl.pallas_call(
        _merge_kernel,
        grid=(m // tm,),
        in_specs=[pl.BlockSpec((tm, d), row), pl.BlockSpec((tm, d), row), pl.BlockSpec((tm, d), row),
                  pl.BlockSpec((tm, d), lambda i: (i, P_RB // d)),
                  pl.BlockSpec((tm, d), lambda i: (i, P_GA // d)),
                  pl.BlockSpec((tm, d), lambda i: (i, P_GB // d)),
                  pl.BlockSpec((1, GLA_DV), const), wspec, wspec],
        out_specs=pl.BlockSpec((tm, d), row),
        out_shape=jax.ShapeDtypeStruct((m, d), BF16),
        compiler_params=_params("parallel"),
        name="merge_gates",
    )(o_attn, o_f, o_b, p, p, p, gla_norm.reshape(1, GLA_DV), w_attn_o, w_gla_o)


def _out_proj_kernel(z_ref, w_ref, x_ref, gate_ref, o_ref):
    y = jnp.dot(z_ref[...], w_ref[...], preferred_element_type=F32)
    o_ref[...] = x_ref[...] + gate_ref[0] * y


def _out_proj(z, w_out, x2, gate, rows_per_batch, tm=512):
    m, d = x2.shape
    tm = min(tm, rows_per_batch)
    tpb = rows_per_batch // tm
    return pl.pallas_call(
        _out_proj_kernel,
        grid=(m // tm,),
        in_specs=[pl.BlockSpec((tm, d), lambda i: (i, 0)),
                  pl.BlockSpec((d, d), lambda i: (0, 0), pipeline_mode=pl.Buffered(1)),
                  pl.BlockSpec((tm, d), lambda i: (i, 0)),
                  pl.BlockSpec((1, 1, d), lambda i: (i // tpb, 0, 0))],
        out_specs=pl.BlockSpec((tm, d), lambda i: (i, 0)),
        out_shape=jax.ShapeDtypeStruct((m, d), F32),
        compiler_params=_params("parallel"),
        name="out_proj_residual",
    )(z, w_out, x2, gate)


FFN_SUB = 256
FFN_SUBS = 2
FFN_HALO = 16
FFN_OUT_COLS = 512


def _ffn_kernel(x_ref, xp_ref, xn_ref, shift_ref, scale_ref, gate_ref, gain_ref, wu_ref, cw_ref, cb_ref,
                wd_ref, o_ref, h_ref, *, tpb):
    i = pl.program_id(0)
    kk = pl.program_id(1)
    tm = x_ref.shape[0]
    d = x_ref.shape[1]
    ext = tm + 2 * FFN_HALO

    def modnorm(x):
        ms = jnp.mean(x * x, axis=-1, keepdims=True)
        y = x * lax.rsqrt(ms + EPS) * gain_ref[...]
        return y * (1.0 + scale_ref[0]) + shift_ref[0]

    @pl.when(kk == 0)
    def _():
        keep_prev = jnp.where((i % tpb) == 0, 0.0, 1.0)
        keep_next = jnp.where((i % tpb) == tpb - 1, 0.0, 1.0)
        h_ref[0:FFN_HALO, :] = (modnorm(xp_ref[...]) * keep_prev).astype(BF16)
        h_ref[FFN_HALO:FFN_HALO + tm, :] = modnorm(x_ref[...]).astype(BF16)
        h_ref[FFN_HALO + tm:ext, :] = (modnorm(xn_ref[...]) * keep_next).astype(BF16)
        o_ref[...] = jnp.zeros_like(o_ref)

    w2 = 2 * FFN_SUB
    us = [jnp.dot(h_ref[...], wu_ref[:, s * w2:(s + 1) * w2], preferred_element_type=F32)
          for s in range(FFN_SUBS)]
    for s in range(FFN_SUBS):
        u = us[s]
        cols = slice(s * w2, (s + 1) * w2)
        rows = slice(FFN_HALO, FFN_HALO + tm)
        conv = (cw_ref[0:1, cols] * pltpu.roll(u, 1, axis=0)[rows]
                + cw_ref[1:2, cols] * u[rows]
                + cw_ref[2:3, cols] * pltpu.roll(u, ext - 1, axis=0)[rows] + cb_ref[:, cols])
        a = conv[:, :FFN_SUB]
        act = (a * _sigmoid(a) * conv[:, FFN_SUB:]).astype(BF16)
        for nb in range(d // FFN_OUT_COLS):
            oc = slice(nb * FFN_OUT_COLS, (nb + 1) * FFN_OUT_COLS)
            o_ref[:, oc] += jnp.dot(act, wd_ref[s * FFN_SUB:(s + 1) * FFN_SUB, oc],
                                    preferred_element_type=F32)

    @pl.when(kk == pl.num_programs(1) - 1)
    def _():
        o_ref[...] = x_ref[...] + gate_ref[0] * o_ref[...]


def _conv_ffn(x2, shift, scale, gate, gain, w_up_r, conv_w_r, conv_b_r, w_down, rows_per_batch, tm=1024):
    m, d = x2.shape
    dff = w_down.shape[0]
    tm = min(tm, rows_per_batch)
    tpb = rows_per_batch // tm
    hb = tm // FFN_HALO
    nhalo = m // FFN_HALO
    tk = FFN_SUB * FFN_SUBS
    nk = dff // tk
    once = pl.Buffered(1)
    return pl.pallas_call(
        functools.partial(_ffn_kernel, tpb=tpb),
        grid=(m // tm, nk),
        in_specs=[pl.BlockSpec((tm, d), lambda i, k: (i, 0), pipeline_mode=once),
                  pl.BlockSpec((FFN_HALO, d), lambda i, k: (jnp.maximum(i * hb - 1, 0), 0)),
                  pl.BlockSpec((FFN_HALO, d), lambda i, k: (jnp.minimum((i + 1) * hb, nhalo - 1), 0)),
                  pl.BlockSpec((1, 1, d), lambda i, k: (i // tpb, 0, 0)),
                  pl.BlockSpec((1, 1, d), lambda i, k: (i // tpb, 0, 0)),
                  pl.BlockSpec((1, 1, d), lambda i, k: (i // tpb, 0, 0)),
                  pl.BlockSpec((1, d), lambda i, k: (0, 0)),
                  pl.BlockSpec((d, 2 * tk), lambda i, k: (0, k)),
                  pl.BlockSpec((3, 2 * tk), lambda i, k: (0, k)),
                  pl.BlockSpec((1, 2 * tk), lambda i, k: (0, k)),
                  pl.BlockSpec((tk, d), lambda i, k: (k, 0))],
        out_specs=pl.BlockSpec((tm, d), lambda i, k: (i, 0)),
        out_shape=jax.ShapeDtypeStruct((m, d), F32),
        scratch_shapes=[pltpu.VMEM((tm + 2 * FFN_HALO, d), BF16)],
        compiler_params=_params("parallel", "arbitrary"),
        name="conv_ffn",
    )(x2, x2, x2, shift, scale, gate, gain.reshape(1, d), w_up_r, conv_w_r, conv_b_r, w_down)


def _regroup_ffn_cols(a):
    lead = a.shape[:-1]
    nblk = D_FF // FFN_SUB
    a = a.reshape(*lead, 2, nblk, FFN_SUB)
    return jnp.swapaxes(a, -3, -2).reshape(*lead, 2 * D_FF)


def _rope_tables(n):
    rows = n // GRID_W
    row = jnp.repeat(jnp.arange(rows), GRID_W)
    col = jnp.tile(jnp.arange(GRID_W), rows)
    n_freq = HEAD_DIM // 4
    inv = ROPE_THETA ** (-jnp.arange(n_freq, dtype=F32) / n_freq)
    ang = jnp.concatenate([row[:, None] * inv, col[:, None] * inv], axis=-1)
    cos, sin = jnp.cos(ang), jnp.sin(ang)
    return jnp.concatenate([cos, cos], axis=-1), jnp.concatenate([-sin, sin], axis=-1)


def _split_w_in(w_in):
    splits = (ATTN_WIDTH, KV_WIDTH, KV_WIDTH, GLA_K_WIDTH, GLA_K_WIDTH, GLA_V_WIDTH, GLA_V_WIDTH,
              GLA_LOWRANK, GLA_LOWRANK, D_MODEL, D_MODEL)
    offs = np.cumsum(splits)[:-1].tolist()
    qa, ka, va, qb, kb, vb, rb, lrf, lrb, ga, gb = jnp.split(w_in, offs, axis=-1)
    w_qk = jnp.concatenate([qa, ka], axis=-1).astype(BF16)
    w_p = jnp.concatenate([qb, kb, vb, rb, ga, gb, va], axis=-1).astype(BF16)
    pad = jnp.zeros((w_in.shape[0], LANES - 2 * GLA_LOWRANK), w_in.dtype)
    w_lr = jnp.concatenate([lrf, lrb, pad], axis=-1).astype(BF16)
    return w_qk, w_p, w_lr


def kernel(x, c, ctx, c_ctx, w_mod, b_mod, g_mix, w_in, q_norm, k_norm, attn_sink, w_gate_f, b_gate_f,
           w_gate_b, b_gate_b, gla_norm, w_attn_o, w_gla_o, w_out, g_ffn, w_up, conv_w, conv_b, w_down):
    batch, seq, d = x.shape
    ctx_len = ctx.shape[1]
    assert w_mod.shape[0] == 1, "single-layer kernel"
    assert d == D_MODEL and seq % GLA_CHUNK == 0 and ctx_len % GLA_CHUNK == 0 and batch <= 7

    cc = jnp.zeros((8, d), F32).at[:batch].set(c).at[batch].set(c_ctx)
    mod = _modulation(cc, w_mod[0], b_mod[0]).reshape(8, 6, d)
    mod_x = [mod[:batch, jj][:, None, :] for jj in range(6)]
    mod_c = [mod[batch:batch + 1, jj][:, None, :] for jj in range(6)]

    w_qk, w_p, w_lr = _split_w_in(w_in[0])
    q_fold = HEAD_DIM ** -0.5 * LOG2E
    qk_norm_w = jnp.concatenate([jnp.tile(q_norm[0] * q_fold, N_Q_HEADS), jnp.tile(k_norm[0], N_KV_HEADS)])
    p_scale = jnp.ones((P_WIDTH,), F32).at[P_QB:P_QB + GLA_K_WIDTH].set(GLA_DK ** -0.5)
    cos2, sin2 = _rope_tables(seq)

    x2 = x.reshape(batch * seq, d)
    c2 = ctx.reshape(batch * ctx_len, d)
    qk, lr = _norm_matmul(x2, mod_x[0], mod_x[1], g_mix[0], w_qk, qk_norm_w, mode="qk",
                          rows_per_batch=seq, cos2=cos2, sin2=sin2, w_lr=w_lr, tm=512, tn=QK_WIDTH)
    p = _norm_matmul(x2, mod_x[0], mod_x[1], g_mix[0], w_p, p_scale, mode="plain", rows_per_batch=seq,
                     tn=1536)
    qk_c, lr_c = _norm_matmul(c2, mod_c[0], mod_c[1], g_mix[0], w_qk, qk_norm_w, mode="qk",
                              rows_per_batch=batch * ctx_len, w_lr=w_lr, tm=512, tn=QK_WIDTH)
    p_c = _norm_matmul(c2, mod_c[0], mod_c[1], g_mix[0], w_p, p_scale, mode="plain",
                       rows_per_batch=batch * ctx_len)

    o_attn = _attention(qk, p, qk_c, p_c, attn_sink[0], batch, seq, ctx_len)

    wgf = jnp.zeros((LANES, GLA_K_WIDTH), F32).at[:GLA_LOWRANK].set(w_gate_f[0])
    wgb = jnp.zeros((LANES, GLA_K_WIDTH), F32).at[GLA_LOWRANK:2 * GLA_LOWRANK].set(w_gate_b[0])
    bgf = b_gate_f[0].reshape(1, GLA_K_WIDTH)
    bgb = b_gate_b[0].reshape(1, GLA_K_WIDTH)
    zero_state = jnp.zeros((batch, GLA_HEADS, GLA_DV, GLA_DK), F32)
    _, _, sf, sb = _gla(p_c, lr_c, wgf, bgf, wgb, bgb, zero_state, zero_state, batch, ctx_len)
    o_f, o_b, _, _ = _gla(p, lr, wgf, bgf, wgb, bgb, sf, sb, batch, seq)

    z = _merge(o_attn, o_f, o_b, p, gla_norm[0], w_attn_o[0].astype(BF16), w_gla_o[0].astype(BF16))
    x1 = _out_proj(z, w_out[0].astype(BF16), x2, mod_x[2], seq)

    out = _conv_ffn(x1, mod_x[3], mod_x[4], mod_x[5], g_ffn[0], _regroup_ffn_cols(w_up[0]).astype(BF16),
                    _regroup_ffn_cols(conv_w[0]), _regroup_ffn_cols(conv_b[0]).reshape(1, 2 * D_FF),
                    w_down[0].astype(BF16), seq)
    return out.reshape(batch, seq, d)
```

```python
import functools

import numpy as np
import jax
import jax.numpy as jnp
from jax import lax
from jax.experimental import pallas as pl
from jax.experimental.pallas import tpu as pltpu

F32 = jnp.float32
BF16 = jnp.bfloat16

D_MODEL = 2048
GRID_W = 64
HEAD_DIM = 128
N_Q_HEADS = 16
N_KV_HEADS = 4
Q_PER_KV = N_Q_HEADS // N_KV_HEADS
WINDOW = 128
ROPE_THETA = 10000.0
GLA_HEADS = 4
GLA_DK = D_MODEL // 2 // GLA_HEADS
GLA_DV = D_MODEL // GLA_HEADS
GLA_LOWRANK = 16
GLA_GATE_NORM = 16.0
D_FF = 5632
EPS = 1e-6
ATTN_WIDTH = N_Q_HEADS * HEAD_DIM
KV_WIDTH = N_KV_HEADS * HEAD_DIM
GLA_K_WIDTH = GLA_HEADS * GLA_DK
GLA_V_WIDTH = GLA_HEADS * GLA_DV

LANES = 128
MXU_WIDTH = 256
VMEM_LIMIT_BYTES = 56 * 1024 * 1024

GLA_CHUNK = 128
GLA_SUB = 8
GLA_LEVELS = (64, 32, 16, 8)
LOG2E = 1.4426950408889634
GLA_HEADS_PER_STEP = 4

P_QB, P_KB, P_VB, P_RB, P_GA, P_GB, P_VA = 0, 1024, 2048, 4096, 6144, 8192, 10240
P_WIDTH = 10752
QK_WIDTH = ATTN_WIDTH + KV_WIDTH
QK_GROUP = 512
STAGE_COLS = 512


def _params(*sem):
    return pltpu.CompilerParams(dimension_semantics=sem, vmem_limit_bytes=VMEM_LIMIT_BYTES)


def _sigmoid(x):
    return 1.0 / (1.0 + jnp.exp(-x))


def _permute_cast_kernel(src_ref, shf_ref, a_ref, b_ref, o_ref, *, shift):
    j = pl.program_id(0)
    bw = o_ref.shape[1]

    @pl.when(shf_ref[j] == 0)
    def _():
        o_ref[...] = a_ref[...].astype(o_ref.dtype)

    @pl.when(shf_ref[j] != 0)
    def _():
        lane = lax.broadcasted_iota(jnp.int32, a_ref.shape, 1)
        lo = pltpu.roll(a_ref[...], bw - shift, axis=1)
        hi = pltpu.roll(b_ref[...], bw - shift, axis=1)
        o_ref[...] = jnp.where(lane < bw - shift, lo, hi).astype(o_ref.dtype)


def _permute_cast_cols(w, src_blocks, shifted, bw, shift=0):
    kdim = w.shape[0]
    n = len(src_blocks)
    src = jnp.asarray(np.asarray(src_blocks, np.int32))
    shf = jnp.asarray(np.asarray(shifted, np.int32))
    return pl.pallas_call(
        functools.partial(_permute_cast_kernel, shift=shift),
        grid_spec=pltpu.PrefetchScalarGridSpec(
            num_scalar_prefetch=2,
            grid=(n,),
            in_specs=[pl.BlockSpec((kdim, bw), lambda j, s, f: (0, s[j])),
                      pl.BlockSpec((kdim, bw), lambda j, s, f: (0, jnp.where(f[j] != 0, s[j] + 1, 0)))],
            out_specs=pl.BlockSpec((kdim, bw), lambda j, s, f: (0, j)),
        ),
        out_shape=jax.ShapeDtypeStruct((kdim, n * bw), BF16),
        compiler_params=_params("arbitrary"),
        name="permute_cast",
    )(src, shf, w, w)


def _mod_kernel(a_ref, w_ref, b_ref, o_ref):
    a = a_ref[...]
    s = (a * _sigmoid(a)).astype(BF16)
    o_ref[...] = jnp.dot(s, w_ref[...].astype(BF16), preferred_element_type=F32) + b_ref[...]


def _modulation(cc, w_mod, b_mod):
    d, n = w_mod.shape
    tn = 1024
    return pl.pallas_call(
        _mod_kernel,
        grid=(n // tn,),
        in_specs=[pl.BlockSpec((8, d), lambda j: (0, 0)),
                  pl.BlockSpec((d, tn), lambda j: (0, j)),
                  pl.BlockSpec((1, tn), lambda j: (0, j))],
        out_specs=pl.BlockSpec((8, tn), lambda j: (0, j)),
        out_shape=jax.ShapeDtypeStruct((8, n), F32),
        compiler_params=_params("parallel"),
        name="modulation",
    )(cc, w_mod, b_mod.reshape(1, n))


def _norm_mm_kernel(*refs, mode, rope, has_lr, tn):
    it = iter(refs)
    x_ref, shift_ref, scale_ref, gain_ref, w_ref, cs_ref = [next(it) for _ in range(6)]
    cos_ref = sin_ref = wlr_ref = lr_ref = seg_ref = perm_ref = None
    if mode == "qk":
        seg_ref, perm_ref = next(it), next(it)
    if rope:
        cos_ref, sin_ref = next(it), next(it)
    if has_lr:
        wlr_ref = next(it)
    o_ref = next(it)
    if has_lr:
        lr_ref = next(it)
    h_ref = next(it)

    @pl.when(pl.program_id(1) == 0)
    def _():
        x = x_ref[...]
        ms = jnp.mean(x * x, axis=-1, keepdims=True)
        y = x * lax.rsqrt(ms + EPS) * gain_ref[...]
        h = (y * (1.0 + scale_ref[0]) + shift_ref[0]).astype(BF16)
        h_ref[...] = h
        if has_lr:
            lr_ref[...] = jnp.dot(h, wlr_ref[...], preferred_element_type=F32)

    if mode == "plain":
        acc = jnp.dot(h_ref[...], w_ref[...], preferred_element_type=F32)
        o_ref[...] = (acc * cs_ref[...]).astype(o_ref.dtype)
    else:
        for gb in range(tn // QK_GROUP):
            big = jnp.dot(h_ref[...], w_ref[:, gb * QK_GROUP:(gb + 1) * QK_GROUP],
                          preferred_element_type=F32)
            for cb in range(QK_GROUP // MXU_WIDTH):
                c0 = gb * QK_GROUP + cb * MXU_WIDTH
                acc = big[:, cb * MXU_WIDTH:(cb + 1) * MXU_WIDTH]
                ss = jnp.dot((acc * acc).astype(BF16), seg_ref[...], preferred_element_type=F32)
                an = acc * lax.rsqrt(ss * (1.0 / HEAD_DIM) + EPS) * cs_ref[:, c0:c0 + MXU_WIDTH]
                if rope:
                    rot = jnp.dot(an.astype(BF16), perm_ref[...], preferred_element_type=F32)
                for hh in range(MXU_WIDTH // HEAD_DIM):
                    hs = slice(hh * HEAD_DIM, (hh + 1) * HEAD_DIM)
                    a = an[:, hs]
                    if rope:
                        a = a * cos_ref[...] + rot[:, hs] * sin_ref[...]
                    o_ref[:, c0 + hh * HEAD_DIM:c0 + (hh + 1) * HEAD_DIM] = a.astype(o_ref.dtype)


def _norm_matmul(x2, shift, scale, gain, w, colvec, *, mode, rows_per_batch, cos2=None, sin2=None,
                 w_lr=None, tm=1024, tn=512):
    m, d = x2.shape
    n = w.shape[1]
    tm = min(tm, rows_per_batch)
    tpb = rows_per_batch // tm
    rope = cos2 is not None
    has_lr = w_lr is not None
    in_specs = [pl.BlockSpec((tm, d), lambda i, j: (i, 0)),
                pl.BlockSpec((1, 1, d), lambda i, j: (i // tpb, 0, 0)),
                pl.BlockSpec((1, 1, d), lambda i, j: (i // tpb, 0, 0)),
                pl.BlockSpec((1, d), lambda i, j: (0, 0)),
                pl.BlockSpec((d, tn), lambda i, j: (0, j)),
                pl.BlockSpec((1, tn), lambda i, j: (0, j))]
    args = [x2, shift, scale, gain.reshape(1, d), w, colvec.reshape(1, n)]
    if mode == "qk":
        lane = np.arange(MXU_WIDTH)
        same_head = lane[:, None] // HEAD_DIM == lane[None, :] // HEAD_DIM
        rolled = (lane[:, None] % HEAD_DIM) == ((lane[None, :] - HEAD_DIM // 2) % HEAD_DIM)
        in_specs += [pl.BlockSpec((MXU_WIDTH, MXU_WIDTH), lambda i, j: (0, 0))] * 2
        args += [jnp.asarray(same_head, BF16), jnp.asarray(same_head & rolled, BF16)]
    if rope:
        in_specs += [pl.BlockSpec((tm, HEAD_DIM), lambda i, j: (i % tpb, 0))] * 2
        args += [cos2, sin2]
    out_shape = [jax.ShapeDtypeStruct((m, n), BF16)]
    out_specs = [pl.BlockSpec((tm, tn), lambda i, j: (i, j))]
    if has_lr:
        in_specs.append(pl.BlockSpec((d, LANES), lambda i, j: (0, 0)))
        args.append(w_lr)
        out_shape.append(jax.ShapeDtypeStruct((m, LANES), F32))
        out_specs.append(pl.BlockSpec((tm, LANES), lambda i, j: (i, 0)))
    res = pl.pallas_call(
        functools.partial(_norm_mm_kernel, mode=mode, rope=rope, has_lr=has_lr, tn=tn),
        grid=(m // tm, n // tn),
        in_specs=in_specs,
        out_specs=out_specs,
        out_shape=out_shape,
        scratch_shapes=[pltpu.VMEM((tm, d), BF16)],
        compiler_params=_params("parallel", "arbitrary"),
        name="norm_matmul_" + mode,
    )(*args)
    return res if has_lr else res[0]


def _attn_kernel(sink_ref, q_ref, kp_ref, kc_ref, kn_ref, kx_ref, vp_ref, vc_ref, vn_ref, vx_ref,
                 o_ref, *, nblk):
    i = pl.program_id(1)
    blk = HEAD_DIM
    dn = (((1,), (1,)), ((), ()))
    row = lax.broadcasted_iota(jnp.int32, (blk, blk), 0)
    col = lax.broadcasted_iota(jnp.int32, (blk, blk), 1)
    ninf = jnp.float32(-jnp.inf)
    bias_p = jnp.where(col >= row, 0.0, ninf) + jnp.where(i > 0, 0.0, ninf)
    bias_n = jnp.where(col <= row, 0.0, ninf) + jnp.where(i < nblk - 1, 0.0, ninf)
    bias_p = jnp.concatenate([bias_p] * Q_PER_KV, axis=0)
    bias_n = jnp.concatenate([bias_n] * Q_PER_KV, axis=0)
    n_ctx = kx_ref.shape[0] // blk

    for h in range(N_KV_HEADS):
        hs = slice(h * HEAD_DIM, (h + 1) * HEAD_DIM)
        qs = jnp.concatenate(
            [q_ref[:, (h * Q_PER_KV + g) * HEAD_DIM:(h * Q_PER_KV + g + 1) * HEAD_DIM]
             for g in range(Q_PER_KV)], axis=0)

        def scores(k_ref):
            return lax.dot_general(qs, k_ref[:, hs], dn, preferred_element_type=F32)

        s_x = scores(kx_ref)
        pieces = [scores(kp_ref) + bias_p, scores(kc_ref), scores(kn_ref) + bias_n]
        pieces += [s_x[:, j * blk:(j + 1) * blk] for j in range(n_ctx)]
        sink = jnp.concatenate(
            [jnp.full((blk, 1), sink_ref[h * Q_PER_KV + g] * LOG2E, F32) for g in range(Q_PER_KV)], axis=0)
        mx = pieces[0]
        for s in pieces[1:]:
            mx = jnp.maximum(mx, s)
        m = jnp.maximum(jnp.max(mx, axis=-1, keepdims=True), sink)
        probs = [jnp.exp2(s - m) for s in pieces]
        psum = probs[0]
        for pr in probs[1:]:
            psum = psum + pr
        denom = jnp.exp2(sink - m) + jnp.sum(psum, axis=-1, keepdims=True)
        p_x = jnp.concatenate(probs[3:], axis=1) if n_ctx > 1 else probs[3]
        o = (jnp.dot(probs[0].astype(BF16), vp_ref[:, hs], preferred_element_type=F32)
             + jnp.dot(probs[1].astype(BF16), vc_ref[:, hs], preferred_element_type=F32)
             + jnp.dot(probs[2].astype(BF16), vn_ref[:, hs], preferred_element_type=F32)
             + jnp.dot(p_x.astype(BF16), vx_ref[:, hs], preferred_element_type=F32))
        o = o / denom
        for g in range(Q_PER_KV):
            c0 = (h * Q_PER_KV + g) * HEAD_DIM
            o_ref[:, c0:c0 + HEAD_DIM] = o[g * blk:(g + 1) * blk].astype(o_ref.dtype)


def _attention(qk, p, qk_c, p_c, sink, batch, seq, ctx_len):
    blk = HEAD_DIM
    nblk = seq // blk
    kcol = ATTN_WIDTH // KV_WIDTH
    vcol = P_VA // KV_WIDTH

    def kv_spec(col0, shift):
        def imap(b, i):
            return (b * nblk + jnp.clip(i + shift, 0, nblk - 1), col0)
        return pl.BlockSpec((blk, KV_WIDTH), imap)

    in_specs = [pl.BlockSpec(memory_space=pltpu.SMEM),
                pl.BlockSpec((blk, ATTN_WIDTH), lambda b, i: (b * nblk + i, 0)),
                kv_spec(kcol, -1), kv_spec(kcol, 0), kv_spec(kcol, 1),
                pl.BlockSpec((ctx_len, KV_WIDTH), lambda b, i: (b, kcol)),
                kv_spec(vcol, -1), kv_spec(vcol, 0), kv_spec(vcol, 1),
                pl.BlockSpec((ctx_len, KV_WIDTH), lambda b, i: (b, vcol))]
    return pl.pallas_call(
        functools.partial(_attn_kernel, nblk=nblk),
        grid=(batch, nblk),
        in_specs=in_specs,
        out_specs=pl.BlockSpec((blk, ATTN_WIDTH), lambda b, i: (b * nblk + i, 0)),
        out_shape=jax.ShapeDtypeStruct((batch * seq, ATTN_WIDTH), BF16),
        compiler_params=_params("parallel", "arbitrary"),
        name="window_attention",
    )(sink, qk, qk, qk, qk, qk_c, p, p, p, p_c)


def _gla_masks():
    c = GLA_CHUNK
    t = np.arange(c)[:, None]
    s = np.arange(c)[None, :]
    fwd = []
    for w in GLA_LEVELS:
        fwd.append((t // (2 * w) == s // (2 * w)) & (t % (2 * w) >= w) & (s % (2 * w) < w))
    fwd.append((t // GLA_SUB == s // GLA_SUB) & (s <= t))
    fwd.append(s <= t)
    fwd = np.stack(fwd).astype(np.float32)
    return np.stack([fwd, np.transpose(fwd, (0, 2, 1))])


def _gla_chunk(q_ref, k_ref, v_ref, lr_ref, wg_ref, bg_ref, mask_ref, st_ref, w_ref, b_ref, o_ref, reverse, hp):
    c = GLA_CHUNK
    nl = len(GLA_LEVELS)
    ksl = slice(hp * GLA_DK, (hp + 1) * GLA_DK)
    vsl = slice(hp * GLA_DV, (hp + 1) * GLA_DV)
    q = q_ref[:, ksl].astype(F32)
    kb = k_ref[:, ksl]
    k = kb.astype(F32)
    v = v_ref[:, vsl]
    xg = jnp.dot(lr_ref[...], wg_ref[:, ksl], preferred_element_type=F32,
                 precision=lax.Precision.HIGHEST) + bg_ref[:, ksl]
    yield
    g = (jnp.minimum(xg, 0.0) - jnp.log1p(jnp.exp(-jnp.abs(xg)))) * (LOG2E / GLA_GATE_NORM)

    tri = mask_ref[nl + 1].astype(BF16)
    g_hi = g.astype(BF16)
    r1 = g - g_hi.astype(F32)
    g_mid = r1.astype(BF16)
    g_lo = (r1 - g_mid.astype(F32)).astype(BF16)
    yield
    b = (jnp.dot(tri, g_hi, preferred_element_type=F32) + jnp.dot(tri, g_mid, preferred_element_type=F32)
         + jnp.dot(tri, g_lo, preferred_element_type=F32))
    b_ref[...] = b
    yield

    dn_nt = (((1,), (1,)), ((), ()))
    a_mat = None
    for lvl, w in enumerate(GLA_LEVELS):
        zero = jnp.zeros((w, GLA_DK), F32)
        qparts, kparts = [], []
        for mblk in range(c // (2 * w)):
            lo, mid, hi = mblk * 2 * w, mblk * 2 * w + w, mblk * 2 * w + 2 * w
            if reverse:
                r = b[mid:mid + 1, :]
                qparts += [q[lo:mid] * jnp.exp2(b[lo:mid] - r), zero]
                kparts += [zero, k[mid:hi] * jnp.exp2(r - b[mid:hi])]
            else:
                r = b[mid - 1:mid, :]
                qparts += [zero, q[mid:hi] * jnp.exp2(b[mid:hi] - r)]
                kparts += [k[lo:mid] * jnp.exp2(r - b[lo:mid]), zero]
        qn = jnp.concatenate(qparts, axis=0).astype(BF16)
        kn = jnp.concatenate(kparts, axis=0).astype(BF16)
        term = lax.dot_general(qn, kn, dn_nt, preferred_element_type=F32)
        if 2 * w < c:
            term = term * mask_ref[lvl]
        a_mat = term if a_mat is None else a_mat + term
        yield

    for i in range(c // GLA_SUB):
        r0 = i * GLA_SUB
        qi = q[r0:r0 + GLA_SUB]
        bi = b[r0:r0 + GLA_SUB]
        for s in range(0, GLA_SUB, 2):
            e0 = jnp.exp2(bi - b_ref[pl.ds(r0 + s, 1), :])
            e1 = jnp.exp2(bi - b_ref[pl.ds(r0 + s + 1, 1), :])
            w_ref[pl.ds((r0 + s) * GLA_SUB, 2 * GLA_SUB), :] = jnp.concatenate(
                [qi * e0, qi * e1], axis=0).astype(BF16)
        yield
    red = lax.dot_general(w_ref[...], kb, dn_nt, preferred_element_type=F32)
    yield
    lane = lax.broadcasted_iota(jnp.int32, (GLA_SUB, c), 1)
    rows = []
    for i in range(c // GLA_SUB):
        blk = jnp.zeros((GLA_SUB, c), F32)
        for s in range(GLA_SUB):
            col = i * GLA_SUB + s
            blk = jnp.where(lane == col, red[col * GLA_SUB:(col + 1) * GLA_SUB], blk)
        rows.append(blk)
        if i % 4 == 3:
            yield
    a_mat = a_mat + jnp.where(mask_ref[nl] > 0.5, jnp.concatenate(rows, axis=0), 0.0)

    st = st_ref[...]
    qe = (q * jnp.exp2(b)).astype(BF16)
    o = lax.dot_general(qe, st.astype(BF16), dn_nt, preferred_element_type=F32)
    yield
    o = o + jnp.dot(a_mat.astype(BF16), v, preferred_element_type=F32)
    o_ref[:, vsl] = o.astype(o_ref.dtype)
    yield

    r_end = 0 if reverse else c - 1
    b_end = b[r_end:r_end + 1, :]
    ke = (k * jnp.exp2(b_end - b)).astype(BF16)
    upd = lax.dot_general(v, ke, (((0,), (0,)), ((), ())), preferred_element_type=F32)
    yield
    st_ref[...] = st * jnp.exp2(b_end) + upd


def _gla_kernel(qf_ref, kf_ref, vf_ref, lrf_ref, qb_ref, kb_ref, vb_ref, lrb_ref,
                wgf_ref, bgf_ref, wgb_ref, bgb_ref, mask_ref, s0f_ref, s0b_ref,
                of_ref, ob_ref, sf_ref, sb_ref, stf_ref, stb_ref, wf_ref, wb_ref, bf_ref, bb_ref):
    cidx = pl.program_id(2)

    @pl.when(cidx == 0)
    def _():
        stf_ref[...] = s0f_ref[0]
        stb_ref[...] = s0b_ref[0]

    chains = []
    for hp in range(GLA_HEADS_PER_STEP):
        chains.append(_gla_chunk(qf_ref, kf_ref, vf_ref, lrf_ref, wgf_ref, bgf_ref, mask_ref.at[0],
                                 stf_ref.at[hp], wf_ref.at[hp], bf_ref.at[hp], of_ref, False, hp))
        chains.append(_gla_chunk(qb_ref, kb_ref, vb_ref, lrb_ref, wgb_ref, bgb_ref, mask_ref.at[1],
                                 stb_ref.at[hp], wb_ref.at[hp], bb_ref.at[hp], ob_ref, True, hp))
    while chains:
        chains = [ch for ch in chains if next(ch, True) is None]

    @pl.when(cidx == pl.num_programs(2) - 1)
    def _():
        sf_ref[0] = stf_ref[...]
        sb_ref[0] = stb_ref[...]


def _gla(p, lr, wgf, bgf, wgb, bgb, s0f, s0b, batch, seq):
    c = GLA_CHUNK
    nc = seq // c
    hps = GLA_HEADS_PER_STEP
    kw, vw = hps * GLA_DK, hps * GLA_DV
    kcol = P_KB // kw
    vcol = P_VB // vw

    def fwd(b, h, i):
        return b * nc + i

    def bwd(b, h, i):
        return b * nc + (nc - 1 - i)

    def data_specs(rowf):
        return [pl.BlockSpec((c, kw), lambda b, h, i: (rowf(b, h, i), h)),
                pl.BlockSpec((c, kw), lambda b, h, i: (rowf(b, h, i), kcol + h)),
                pl.BlockSpec((c, vw), lambda b, h, i: (rowf(b, h, i), vcol + h)),
                pl.BlockSpec((c, LANES), lambda b, h, i: (rowf(b, h, i), 0))]

    gate_specs = [pl.BlockSpec((LANES, kw), lambda b, h, i: (0, h)),
                  pl.BlockSpec((1, kw), lambda b, h, i: (0, h))]
    state_spec = pl.BlockSpec((1, hps, GLA_DV, GLA_DK), lambda b, h, i: (b, h, 0, 0))
    masks = jnp.asarray(_gla_masks())
    in_specs = (data_specs(fwd) + data_specs(bwd) + gate_specs + gate_specs
                + [pl.BlockSpec(masks.shape, lambda b, h, i: (0, 0, 0, 0)), state_spec, state_spec])
    out_specs = [pl.BlockSpec((c, vw), lambda b, h, i: (fwd(b, h, i), h)),
                 pl.BlockSpec((c, vw), lambda b, h, i: (bwd(b, h, i), h)),
                 state_spec, state_spec]
    out_shape = [jax.ShapeDtypeStruct((batch * seq, GLA_V_WIDTH), BF16)] * 2 + [
        jax.ShapeDtypeStruct((batch, GLA_HEADS, GLA_DV, GLA_DK), F32)] * 2
    return pl.pallas_call(
        _gla_kernel,
        grid=(batch, GLA_HEADS // hps, nc),
        in_specs=in_specs,
        out_specs=out_specs,
        out_shape=out_shape,
        scratch_shapes=[pltpu.VMEM((hps, GLA_DV, GLA_DK), F32), pltpu.VMEM((hps, GLA_DV, GLA_DK), F32),
                        pltpu.VMEM((hps, c * GLA_SUB, GLA_DK), BF16),
                        pltpu.VMEM((hps, c * GLA_SUB, GLA_DK), BF16),
                        pltpu.VMEM((hps, c, GLA_DK), F32), pltpu.VMEM((hps, c, GLA_DK), F32)],
        compiler_params=_params("parallel", "parallel", "arbitrary"),
        name="gla_scan",
    )(p, p, p, lr, p, p, p, lr, wgf, bgf, wgb, bgb, masks, s0f, s0b)


def _merge_kernel(oa_ref, of_ref, ob_ref, rb_ref, ga_ref, gb_ref, gn_ref, wa_ref, wg_ref, z_ref):
    og = of_ref[...].astype(F32) + ob_ref[...].astype(F32)
    parts = []
    for hh in range(GLA_HEADS):
        a = og[:, hh * GLA_DV:(hh + 1) * GLA_DV]
        ms = jnp.mean(a * a, axis=-1, keepdims=True)
        parts.append(a * lax.rsqrt(ms + EPS) * gn_ref[...])
    n = jnp.concatenate(parts, axis=1)
    rb = rb_ref[...].astype(F32)
    n = (n * (rb * _sigmoid(rb))).astype(BF16)
    y_gla = jnp.dot(n, wg_ref[...], preferred_element_type=F32)
    y_att = jnp.dot(oa_ref[...], wa_ref[...], preferred_element_type=F32)
    z = _sigmoid(ga_ref[...].astype(F32)) * y_att + _sigmoid(gb_ref[...].astype(F32)) * y_gla
    z_ref[...] = z.astype(z_ref.dtype)


def _merge(o_attn, o_f, o_b, p, gla_norm, w_attn_o, w_gla_o, tm=256):
    m, d = o_attn.shape
    row = lambda i: (i, 0)
    const = lambda i: (0, 0)
    wspec = pl.BlockSpec((d, d), const, pipeline_mode=pl.Buffered(1))
    return pl.pallas_call(
        _merge_kernel,
        grid=(m // tm,),
        in_specs=[pl.BlockSpec((tm, d), row), pl.BlockSpec((tm, d), row), pl.BlockSpec((tm, d), row),
                  pl.BlockSpec((tm, d), lambda i: (i, P_RB // d)),
                  pl.BlockSpec((tm, d), lambda i: (i, P_GA // d)),
                  pl.BlockSpec((tm, d), lambda i: (i, P_GB // d)),
                  pl.BlockSpec((1, GLA_DV), const), wspec, wspec],
        out_specs=pl.BlockSpec((tm, d), row),
        out_shape=jax.ShapeDtypeStruct((m, d), BF16),
        compiler_params=_params("parallel"),
        name="merge_gates",
    )(o_attn, o_f, o_b, p, p, p, gla_norm.reshape(1, GLA_DV), w_attn_o, w_gla_o)


def _out_proj_kernel(z_ref, w_ref, x_ref, gate_ref, o_ref):
    y = jnp.dot(z_ref[...], w_ref[...], preferred_element_type=F32)
    o_ref[...] = x_ref[...] + gate_ref[0] * y


def _out_proj(z, w_out, x2, gate, rows_per_batch, tm=512):
    m, d = x2.shape
    tm = min(tm, rows_per_batch)
    tpb = rows_per_batch // tm
    return pl.pallas_call(
        _out_proj_kernel,
        grid=(m // tm,),
        in_specs=[pl.BlockSpec((tm, d), lambda i: (i, 0)),
                  pl.BlockSpec((d, d), lambda i: (0, 0), pipeline_mode=pl.Buffered(1)),
                  pl.BlockSpec((tm, d), lambda i: (i, 0)),
                  pl.BlockSpec((1, 1, d), lambda i: (i // tpb, 0, 0))],
        out_specs=pl.BlockSpec((tm, d), lambda i: (i, 0)),
        out_shape=jax.ShapeDtypeStruct((m, d), F32),
        compiler_params=_params("parallel"),
        name="out_proj_residual",
    )(z, w_out, x2, gate)


FFN_SUB = 256
FFN_SUBS = 2
FFN_HALO = 16
FFN_OUT_COLS = 512


def _ffn_kernel(x_ref, xp_ref, xn_ref, shift_ref, scale_ref, gate_ref, gain_ref, wu_ref, cw_ref, cb_ref,
                wd_ref, o_ref, h_ref, *, tpb):
    i = pl.program_id(0)
    kk = pl.program_id(1)
    tm = x_ref.shape[0]
    d = x_ref.shape[1]
    ext = tm + 2 * FFN_HALO

    def modnorm(x):
        ms = jnp.mean(x * x, axis=-1, keepdims=True)
        y = x * lax.rsqrt(ms + EPS) * gain_ref[...]
        return y * (1.0 + scale_ref[0]) + shift_ref[0]

    @pl.when(kk == 0)
    def _():
        keep_prev = jnp.where((i % tpb) == 0, 0.0, 1.0)
        keep_next = jnp.where((i % tpb) == tpb - 1, 0.0, 1.0)
        h_ref[0:FFN_HALO, :] = (modnorm(xp_ref[...]) * keep_prev).astype(BF16)
        h_ref[FFN_HALO:FFN_HALO + tm, :] = modnorm(x_ref[...]).astype(BF16)
        h_ref[FFN_HALO + tm:ext, :] = (modnorm(xn_ref[...]) * keep_next).astype(BF16)
        o_ref[...] = jnp.zeros_like(o_ref)

    w2 = 2 * FFN_SUB
    us = [jnp.dot(h_ref[...], wu_ref[:, s * w2:(s + 1) * w2], preferred_element_type=F32)
          for s in range(FFN_SUBS)]
    for s in range(FFN_SUBS):
        u = us[s]
        cols = slice(s * w2, (s + 1) * w2)
        rows = slice(FFN_HALO, FFN_HALO + tm)
        conv = (cw_ref[0:1, cols] * pltpu.roll(u, 1, axis=0)[rows]
                + cw_ref[1:2, cols] * u[rows]
                + cw_ref[2:3, cols] * pltpu.roll(u, ext - 1, axis=0)[rows] + cb_ref[:, cols])
        a = conv[:, :FFN_SUB]
        act = (a * _sigmoid(a) * conv[:, FFN_SUB:]).astype(BF16)
        for nb in range(d // FFN_OUT_COLS):
            oc = slice(nb * FFN_OUT_COLS, (nb + 1) * FFN_OUT_COLS)
            o_ref[:, oc] += jnp.dot(act, wd_ref[s * FFN_SUB:(s + 1) * FFN_SUB, oc],
                                    preferred_element_type=F32)

    @pl.when(kk == pl.num_programs(1) - 1)
    def _():
        o_ref[...] = x_ref[...] + gate_ref[0] * o_ref[...]


def _conv_ffn(x2, shift, scale, gate, gain, w_up_r, conv_w_r, conv_b_r, w_down, rows_per_batch, tm=1024):
    m, d = x2.shape
    dff = w_down.shape[0]
    tm = min(tm, rows_per_batch)
    tpb = rows_per_batch // tm
    hb = tm // FFN_HALO
    nhalo = m // FFN_HALO
    tk = FFN_SUB * FFN_SUBS
    nk = dff // tk
    once = pl.Buffered(1)
    return pl.pallas_call(
        functools.partial(_ffn_kernel, tpb=tpb),
        grid=(m // tm, nk),
        in_specs=[pl.BlockSpec((tm, d), lambda i, k: (i, 0), pipeline_mode=once),
                  pl.BlockSpec((FFN_HALO, d), lambda i, k: (jnp.maximum(i * hb - 1, 0), 0)),
                  pl.BlockSpec((FFN_HALO, d), lambda i, k: (jnp.minimum((i + 1) * hb, nhalo - 1), 0)),
                  pl.BlockSpec((1, 1, d), lambda i, k: (i // tpb, 0, 0)),
                  pl.BlockSpec((1, 1, d), lambda i, k: (i // tpb, 0, 0)),
                  pl.BlockSpec((1, 1, d), lambda i, k: (i // tpb, 0, 0)),
                  pl.BlockSpec((1, d), lambda i, k: (0, 0)),
                  pl.BlockSpec((d, 2 * tk), lambda i, k: (0, k)),
                  pl.BlockSpec((3, 2 * tk), lambda i, k: (0, k)),
                  pl.BlockSpec((1, 2 * tk), lambda i, k: (0, k)),
                  pl.BlockSpec((tk, d), lambda i, k: (k, 0))],
        out_specs=pl.BlockSpec((tm, d), lambda i, k: (i, 0)),
        out_shape=jax.ShapeDtypeStruct((m, d), F32),
        scratch_shapes=[pltpu.VMEM((tm + 2 * FFN_HALO, d), BF16)],
        compiler_params=_params("parallel", "arbitrary"),
        name="conv_ffn",
    )(x2, x2, x2, shift, scale, gate, gain.reshape(1, d), w_up_r, conv_w_r, conv_b_r, w_down)


def _regroup_ffn_cols(a):
    lead = a.shape[:-1]
    nblk = D_FF // FFN_SUB
    a = a.reshape(*lead, 2, nblk, FFN_SUB)
    return jnp.swapaxes(a, -3, -2).reshape(*lead, 2 * D_FF)


def _rope_tables(n):
    rows = n // GRID_W
    row = jnp.repeat(jnp.arange(rows), GRID_W)
    col = jnp.tile(jnp.arange(GRID_W), rows)
    n_freq = HEAD_DIM // 4
    inv = ROPE_THETA ** (-jnp.arange(n_freq, dtype=F32) / n_freq)
    ang = jnp.concatenate([row[:, None] * inv, col[:, None] * inv], axis=-1)
    cos, sin = jnp.cos(ang), jnp.sin(ang)
    return jnp.concatenate([cos, cos], axis=-1), jnp.concatenate([-sin, sin], axis=-1)


def _split_w_in(w_in):
    names = ("qa", "ka", "va", "qb", "kb", "vb", "rb", "lrf", "lrb", "ga", "gb")
    splits = (ATTN_WIDTH, KV_WIDTH, KV_WIDTH, GLA_K_WIDTH, GLA_K_WIDTH, GLA_V_WIDTH, GLA_V_WIDTH,
              GLA_LOWRANK, GLA_LOWRANK, D_MODEL, D_MODEL)
    start = dict(zip(names, np.cumsum((0,) + splits[:-1]).tolist()))
    width = dict(zip(names, splits))
    bw = STAGE_COLS
    shift = 2 * GLA_LOWRANK

    def blocks(group):
        src, shf = [], []
        for nm in group:
            off = start[nm] % bw
            assert off in (0, shift) and width[nm] % bw == 0
            src += [start[nm] // bw + t for t in range(width[nm] // bw)]
            shf += [int(off != 0)] * (width[nm] // bw)
        return src, shf

    w_qk = _permute_cast_cols(w_in, *blocks(("qa", "ka")), bw, shift)
    w_p = _permute_cast_cols(w_in, *blocks(("qb", "kb", "vb", "rb", "ga", "gb", "va")), bw, shift)
    lr0 = start["lrf"]
    pad = jnp.zeros((w_in.shape[0], LANES - shift), w_in.dtype)
    w_lr = jnp.concatenate([w_in[:, lr0:lr0 + shift], pad], axis=-1).astype(BF16)
    return w_qk, w_p, w_lr


def kernel(x, c, ctx, c_ctx, w_mod, b_mod, g_mix, w_in, q_norm, k_norm, attn_sink, w_gate_f, b_gate_f,
           w_gate_b, b_gate_b, gla_norm, w_attn_o, w_gla_o, w_out, g_ffn, w_up, conv_w, conv_b, w_down):
    batch, seq, d = x.shape
    ctx_len = ctx.shape[1]
    assert w_mod.shape[0] == 1, "single-layer kernel"
    assert d == D_MODEL and seq % GLA_CHUNK == 0 and ctx_len % GLA_CHUNK == 0 and batch <= 7

    cc = jnp.zeros((8, d), F32).at[:batch].set(c).at[batch].set(c_ctx)
    mod = _modulation(cc, w_mod[0], b_mod[0]).reshape(8, 6, d)
    mod_x = [mod[:batch, jj][:, None, :] for jj in range(6)]
    mod_c = [mod[batch:batch + 1, jj][:, None, :] for jj in range(6)]

    w_qk, w_p, w_lr = _split_w_in(w_in[0])
    q_fold = HEAD_DIM ** -0.5 * LOG2E
    qk_norm_w = jnp.concatenate([jnp.tile(q_norm[0] * q_fold, N_Q_HEADS), jnp.tile(k_norm[0], N_KV_HEADS)])
    p_scale = jnp.ones((P_WIDTH,), F32).at[P_QB:P_QB + GLA_K_WIDTH].set(GLA_DK ** -0.5)
    cos2, sin2 = _rope_tables(seq)

    x2 = x.reshape(batch * seq, d)
    c2 = ctx.reshape(batch * ctx_len, d)
    qk, lr = _norm_matmul(x2, mod_x[0], mod_x[1], g_mix[0], w_qk, qk_norm_w, mode="qk",
                          rows_per_batch=seq, cos2=cos2, sin2=sin2, w_lr=w_lr, tm=512, tn=QK_WIDTH)
    p = _norm_matmul(x2, mod_x[0], mod_x[1], g_mix[0], w_p, p_scale, mode="plain", rows_per_batch=seq,
                     tn=1536)
    qk_c, lr_c = _norm_matmul(c2, mod_c[0], mod_c[1], g_mix[0], w_qk, qk_norm_w, mode="qk",
                              rows_per_batch=batch * ctx_len, w_lr=w_lr, tm=512, tn=QK_WIDTH)
    p_c = _norm_matmul(c2, mod_c[0], mod_c[1], g_mix[0], w_p, p_scale, mode="plain",
                       rows_per_batch=batch * ctx_len)

    o_attn = _attention(qk, p, qk_c, p_c, attn_sink[0], batch, seq, ctx_len)

    wgf = jnp.zeros((LANES, GLA_K_WIDTH), F32).at[:GLA_LOWRANK].set(w_gate_f[0])
    wgb = jnp.zeros((LANES, GLA_K_WIDTH), F32).at[GLA_LOWRANK:2 * GLA_LOWRANK].set(w_gate_b[0])
    bgf = b_gate_f[0].reshape(1, GLA_K_WIDTH)
    bgb = b_gate_b[0].reshape(1, GLA_K_WIDTH)
    zero_state = jnp.zeros((batch, GLA_HEADS, GLA_DV, GLA_DK), F32)
    _, _, sf, sb = _gla(p_c, lr_c, wgf, bgf, wgb, bgb, zero_state, zero_state, batch, ctx_len)
    o_f, o_b, _, _ = _gla(p, lr, wgf, bgf, wgb, bgb, sf, sb, batch, seq)

    z = _merge(o_attn, o_f, o_b, p, gla_norm[0], w_attn_o[0].astype(BF16), w_gla_o[0].astype(BF16))
    x1 = _out_proj(z, w_out[0].astype(BF16), x2, mod_x[2], seq)

    nblk = D_FF // FFN_SUB
    w_up_r = _permute_cast_cols(w_up[0], [(j % 2) * nblk + j // 2 for j in range(2 * nblk)],
                                [0] * (2 * nblk), FFN_SUB)
    out = _conv_ffn(x1, mod_x[3], mod_x[4], mod_x[5], g_ffn[0], w_up_r,
                    _regroup_ffn_cols(conv_w[0]), _regroup_ffn_cols(conv_b[0]).reshape(1, 2 * D_FF),
                    w_down[0].astype(BF16), seq)
    return out.reshape(batch, seq, d)
```

```python
import functools

import numpy as np
import jax
import jax.numpy as jnp
from jax import lax
from jax.experimental import pallas as pl
from jax.experimental.pallas import tpu as pltpu

F32 = jnp.float32
BF16 = jnp.bfloat16

D_MODEL = 2048
GRID_W = 64
HEAD_DIM = 128
N_Q_HEADS = 16
N_KV_HEADS = 4
Q_PER_KV = N_Q_HEADS // N_KV_HEADS
WINDOW = 128
ROPE_THETA = 10000.0
GLA_HEADS = 4
GLA_DK = D_MODEL // 2 // GLA_HEADS
GLA_DV = D_MODEL // GLA_HEADS
GLA_LOWRANK = 16
GLA_GATE_NORM = 16.0
D_FF = 5632
EPS = 1e-6
ATTN_WIDTH = N_Q_HEADS * HEAD_DIM
KV_WIDTH = N_KV_HEADS * HEAD_DIM
GLA_K_WIDTH = GLA_HEADS * GLA_DK
GLA_V_WIDTH = GLA_HEADS * GLA_DV

LANES = 128
MXU_WIDTH = 256
VMEM_LIMIT_BYTES = 60 * 1024 * 1024

GLA_CHUNK = 128
GLA_SUB = 8
GLA_LEVELS = (64, 32, 16, 8)
LOG2E = 1.4426950408889634
GLA_HEADS_PER_STEP = 4

P_QB, P_KB, P_VB, P_RB, P_GA, P_GB, P_VA = 0, 1024, 2048, 4096, 6144, 8192, 10240
P_WIDTH = 10752
QK_WIDTH = ATTN_WIDTH + KV_WIDTH
QK_GROUP = 512
STAGE_COLS = 512


def _params(*sem):
    return pltpu.CompilerParams(dimension_semantics=sem, vmem_limit_bytes=VMEM_LIMIT_BYTES)


def _sigmoid(x):
    return 1.0 / (1.0 + jnp.exp(-x))


def _permute_cast_kernel(src_ref, a_ref, o_ref):
    o_ref[...] = a_ref[...].astype(o_ref.dtype)


def _permute_cast_cols(w, src_blocks, bw):
    kdim = w.shape[0]
    n = len(src_blocks)
    src = jnp.asarray(np.asarray(src_blocks, np.int32))
    return pl.pallas_call(
        _permute_cast_kernel,
        grid_spec=pltpu.PrefetchScalarGridSpec(
            num_scalar_prefetch=1,
            grid=(n,),
            in_specs=[pl.BlockSpec((kdim, bw), lambda j, s: (0, s[j]))],
            out_specs=pl.BlockSpec((kdim, bw), lambda j, s: (0, j)),
        ),
        out_shape=jax.ShapeDtypeStruct((kdim, n * bw), BF16),
        compiler_params=_params("arbitrary"),
        name="permute_cast",
    )(src, w)


def _transpose_cast_kernel(start_ref, a_ref, o_ref):
    o_ref[...] = a_ref[...].T.astype(o_ref.dtype)


def _transpose_cast_rows(wt, row_starts, bw):
    kdim = wt.shape[1]
    n = len(row_starts)
    sub = 8
    assert all(r % sub == 0 for r in row_starts)
    starts = jnp.asarray(np.asarray(row_starts, np.int32) // sub)
    return pl.pallas_call(
        _transpose_cast_kernel,
        grid_spec=pltpu.PrefetchScalarGridSpec(
            num_scalar_prefetch=1,
            grid=(n,),
            in_specs=[pl.BlockSpec((pl.Element(bw), pl.Element(kdim)), lambda j, s: (s[j] * sub, 0))],
            out_specs=pl.BlockSpec((kdim, bw), lambda j, s: (0, j)),
        ),
        out_shape=jax.ShapeDtypeStruct((kdim, n * bw), BF16),
        compiler_params=_params("arbitrary"),
        name="transpose_cast",
    )(starts, wt)


def _mod_kernel(a_ref, w_ref, b_ref, o_ref):
    a = a_ref[...]
    s = (a * _sigmoid(a)).astype(BF16)
    o_ref[...] = jnp.dot(s, w_ref[...].astype(BF16), preferred_element_type=F32) + b_ref[...]


def _modulation(cc, w_mod, b_mod):
    d, n = w_mod.shape
    tn = 1024
    return pl.pallas_call(
        _mod_kernel,
        grid=(n // tn,),
        in_specs=[pl.BlockSpec((8, d), lambda j: (0, 0)),
                  pl.BlockSpec((d, tn), lambda j: (0, j)),
                  pl.BlockSpec((1, tn), lambda j: (0, j))],
        out_specs=pl.BlockSpec((8, tn), lambda j: (0, j)),
        out_shape=jax.ShapeDtypeStruct((8, n), F32),
        compiler_params=_params("parallel"),
        name="modulation",
    )(cc, w_mod, b_mod.reshape(1, n))


def _norm_mm_kernel(*refs, mode, rope, has_lr, tn):
    it = iter(refs)
    x_ref, shift_ref, scale_ref, gain_ref, w_ref, cs_ref = [next(it) for _ in range(6)]
    cos_ref = sin_ref = wlr_ref = lr_ref = seg_ref = perm_ref = None
    if mode == "qk":
        seg_ref, perm_ref = next(it), next(it)
    if rope:
        cos_ref, sin_ref = next(it), next(it)
    if has_lr:
        wlr_ref = next(it)
    o_ref = next(it)
    if has_lr:
        lr_ref = next(it)
    h_ref = next(it)

    @pl.when(pl.program_id(1) == 0)
    def _():
        x = x_ref[...]
        ms = jnp.mean(x * x, axis=-1, keepdims=True)
        y = x * lax.rsqrt(ms + EPS) * gain_ref[...]
        h = (y * (1.0 + scale_ref[0]) + shift_ref[0]).astype(BF16)
        h_ref[...] = h
        if has_lr:
            lr_ref[...] = jnp.dot(h, wlr_ref[...], preferred_element_type=F32)

    if mode == "plain":
        acc = jnp.dot(h_ref[...], w_ref[...], preferred_element_type=F32)
        o_ref[...] = (acc * cs_ref[...]).astype(o_ref.dtype)
    else:
        for gb in range(tn // QK_GROUP):
            big = jnp.dot(h_ref[...], w_ref[:, gb * QK_GROUP:(gb + 1) * QK_GROUP],
                          preferred_element_type=F32)
            for cb in range(QK_GROUP // MXU_WIDTH):
                c0 = gb * QK_GROUP + cb * MXU_WIDTH
                acc = big[:, cb * MXU_WIDTH:(cb + 1) * MXU_WIDTH]
                ss = jnp.dot((acc * acc).astype(BF16), seg_ref[...], preferred_element_type=F32)
                an = acc * lax.rsqrt(ss * (1.0 / HEAD_DIM) + EPS) * cs_ref[:, c0:c0 + MXU_WIDTH]
                if rope:
                    rot = jnp.dot(an.astype(BF16), perm_ref[...], preferred_element_type=F32)
                for hh in range(MXU_WIDTH // HEAD_DIM):
                    hs = slice(hh * HEAD_DIM, (hh + 1) * HEAD_DIM)
                    a = an[:, hs]
                    if rope:
                        a = a * cos_ref[...] + rot[:, hs] * sin_ref[...]
                    o_ref[:, c0 + hh * HEAD_DIM:c0 + (hh + 1) * HEAD_DIM] = a.astype(o_ref.dtype)


def _norm_matmul(x2, shift, scale, gain, w, colvec, *, mode, rows_per_batch, cos2=None, sin2=None,
                 w_lr=None, tm=1024, tn=512):
    m, d = x2.shape
    n = w.shape[1]
    tm = min(tm, rows_per_batch)
    tpb = rows_per_batch // tm
    rope = cos2 is not None
    has_lr = w_lr is not None
    in_specs = [pl.BlockSpec((tm, d), lambda i, j: (i, 0)),
                pl.BlockSpec((1, 1, d), lambda i, j: (i // tpb, 0, 0)),
                pl.BlockSpec((1, 1, d), lambda i, j: (i // tpb, 0, 0)),
                pl.BlockSpec((1, d), lambda i, j: (0, 0)),
                pl.BlockSpec((d, tn), lambda i, j: (0, j)),
                pl.BlockSpec((1, tn), lambda i, j: (0, j))]
    args = [x2, shift, scale, gain.reshape(1, d), w, colvec.reshape(1, n)]
    if mode == "qk":
        lane = np.arange(MXU_WIDTH)
        same_head = lane[:, None] // HEAD_DIM == lane[None, :] // HEAD_DIM
        rolled = (lane[:, None] % HEAD_DIM) == ((lane[None, :] - HEAD_DIM // 2) % HEAD_DIM)
        in_specs += [pl.BlockSpec((MXU_WIDTH, MXU_WIDTH), lambda i, j: (0, 0))] * 2
        args += [jnp.asarray(same_head, BF16), jnp.asarray(same_head & rolled, BF16)]
    if rope:
        in_specs += [pl.BlockSpec((tm, HEAD_DIM), lambda i, j: (i % tpb, 0))] * 2
        args += [cos2, sin2]
    out_shape = [jax.ShapeDtypeStruct((m, n), BF16)]
    out_specs = [pl.BlockSpec((tm, tn), lambda i, j: (i, j))]
    if has_lr:
        in_specs.append(pl.BlockSpec((d, LANES), lambda i, j: (0, 0)))
        args.append(w_lr)
        out_shape.append(jax.ShapeDtypeStruct((m, LANES), F32))
        out_specs.append(pl.BlockSpec((tm, LANES), lambda i, j: (i, 0)))
    res = pl.pallas_call(
        functools.partial(_norm_mm_kernel, mode=mode, rope=rope, has_lr=has_lr, tn=tn),
        grid=(m // tm, n // tn),
        in_specs=in_specs,
        out_specs=out_specs,
        out_shape=out_shape,
        scratch_shapes=[pltpu.VMEM((tm, d), BF16)],
        compiler_params=_params("parallel", "arbitrary"),
        name="norm_matmul_" + mode,
    )(*args)
    return res if has_lr else res[0]


ATTN_Q_BLOCKS = 2


def _attn_kernel(sink_ref, q_ref, *refs, nblk):
    nq = ATTN_Q_BLOCKS
    k_refs, kx_ref = refs[:nq + 2], refs[nq + 2]
    v_refs, vx_ref = refs[nq + 3:2 * nq + 5], refs[2 * nq + 5]
    o_ref = refs[2 * nq + 6]
    i = pl.program_id(1)
    blk = HEAD_DIM
    dn = (((1,), (1,)), ((), ()))
    row = lax.broadcasted_iota(jnp.int32, (blk, blk), 0)
    col = lax.broadcasted_iota(jnp.int32, (blk, blk), 1)
    ninf = jnp.float32(-jnp.inf)
    tri_p = jnp.concatenate([jnp.where(col >= row, 0.0, ninf)] * Q_PER_KV, axis=0)
    tri_n = jnp.concatenate([jnp.where(col <= row, 0.0, ninf)] * Q_PER_KV, axis=0)
    n_ctx = kx_ref.shape[0] // blk

    for a in range(nq):
        gblk = i * nq + a
        bias_p = tri_p + jnp.where(gblk > 0, 0.0, ninf)
        bias_n = tri_n + jnp.where(gblk < nblk - 1, 0.0, ninf)
        rows = slice(a * blk, (a + 1) * blk)
        for h in range(N_KV_HEADS):
            hs = slice(h * HEAD_DIM, (h + 1) * HEAD_DIM)
            qs = jnp.concatenate(
                [q_ref[rows, (h * Q_PER_KV + g) * HEAD_DIM:(h * Q_PER_KV + g + 1) * HEAD_DIM]
                 for g in range(Q_PER_KV)], axis=0)

            def scores(k_ref):
                return lax.dot_general(qs, k_ref[:, hs], dn, preferred_element_type=F32)

            s_x = scores(kx_ref)
            pieces = [scores(k_refs[a]) + bias_p, scores(k_refs[a + 1]), scores(k_refs[a + 2]) + bias_n]
            pieces += [s_x[:, j * blk:(j + 1) * blk] for j in range(n_ctx)]
            sink = jnp.concatenate(
                [jnp.full((blk, 1), sink_ref[h * Q_PER_KV + g] * LOG2E, F32) for g in range(Q_PER_KV)],
                axis=0)
            mx = pieces[0]
            for s in pieces[1:]:
                mx = jnp.maximum(mx, s)
            m = jnp.maximum(jnp.max(mx, axis=-1, keepdims=True), sink)
            probs = [jnp.exp2(s - m) for s in pieces]
            psum = probs[0]
            for pr in probs[1:]:
                psum = psum + pr
            denom = jnp.exp2(sink - m) + jnp.sum(psum, axis=-1, keepdims=True)
            p_x = jnp.concatenate(probs[3:], axis=1) if n_ctx > 1 else probs[3]
            o = (jnp.dot(probs[0].astype(BF16), v_refs[a][:, hs], preferred_element_type=F32)
                 + jnp.dot(probs[1].astype(BF16), v_refs[a + 1][:, hs], preferred_element_type=F32)
                 + jnp.dot(probs[2].astype(BF16), v_refs[a + 2][:, hs], preferred_element_type=F32)
                 + jnp.dot(p_x.astype(BF16), vx_ref[:, hs], preferred_element_type=F32))
            o = o / denom
            for g in range(Q_PER_KV):
                c0 = (h * Q_PER_KV + g) * HEAD_DIM
                o_ref[rows, c0:c0 + HEAD_DIM] = o[g * blk:(g + 1) * blk].astype(o_ref.dtype)


def _attention(qk, p, qk_c, p_c, sink, batch, seq, ctx_len):
    blk = HEAD_DIM
    nblk = seq // blk
    nq = ATTN_Q_BLOCKS
    nstep = nblk // nq
    kcol = ATTN_WIDTH // KV_WIDTH
    vcol = P_VA // KV_WIDTH

    def kv_specs(col0):
        def spec(shift):
            return pl.BlockSpec(
                (blk, KV_WIDTH), lambda b, i: (b * nblk + jnp.clip(i * nq + shift, 0, nblk - 1), col0))
        return [spec(shift) for shift in range(-1, nq + 1)]

    in_specs = ([pl.BlockSpec(memory_space=pltpu.SMEM),
                 pl.BlockSpec((nq * blk, ATTN_WIDTH), lambda b, i: (b * nstep + i, 0))]
                + kv_specs(kcol) + [pl.BlockSpec((ctx_len, KV_WIDTH), lambda b, i: (b, kcol))]
                + kv_specs(vcol) + [pl.BlockSpec((ctx_len, KV_WIDTH), lambda b, i: (b, vcol))])
    return pl.pallas_call(
        functools.partial(_attn_kernel, nblk=nblk),
        grid=(batch, nstep),
        in_specs=in_specs,
        out_specs=pl.BlockSpec((nq * blk, ATTN_WIDTH), lambda b, i: (b * nstep + i, 0)),
        out_shape=jax.ShapeDtypeStruct((batch * seq, ATTN_WIDTH), BF16),
        compiler_params=_params("parallel", "arbitrary"),
        name="window_attention",
    )(sink, qk, *([qk] * (nq + 2)), qk_c, *([p] * (nq + 2)), p_c)


def _gla_masks():
    c = GLA_CHUNK
    t = np.arange(c)[:, None]
    s = np.arange(c)[None, :]
    fwd = []
    for w in GLA_LEVELS:
        fwd.append((t // (2 * w) == s // (2 * w)) & (t % (2 * w) >= w) & (s % (2 * w) < w))
    fwd.append((t // GLA_SUB == s // GLA_SUB) & (s <= t))
    fwd.append(s <= t)
    fwd = np.stack(fwd).astype(np.float32)
    return np.stack([fwd, np.transpose(fwd, (0, 2, 1))])


def _gla_chunk(q_ref, k_ref, v_ref, lr_ref, wg_ref, bg_ref, mask_ref, st_ref, w_ref, b_ref, o_ref, reverse, hp):
    c = GLA_CHUNK
    nl = len(GLA_LEVELS)
    ksl = slice(hp * GLA_DK, (hp + 1) * GLA_DK)
    vsl = slice(hp * GLA_DV, (hp + 1) * GLA_DV)
    q = q_ref[:, ksl].astype(F32)
    kb = k_ref[:, ksl]
    k = kb.astype(F32)
    v = v_ref[:, vsl]
    xg = jnp.dot(lr_ref[...], wg_ref[:, ksl], preferred_element_type=F32,
                 precision=lax.Precision.HIGHEST) + bg_ref[:, ksl]
    yield
    g = (jnp.minimum(xg, 0.0) - jnp.log1p(jnp.exp(-jnp.abs(xg)))) * (LOG2E / GLA_GATE_NORM)

    tri = mask_ref[nl + 1].astype(BF16)
    g_hi = g.astype(BF16)
    r1 = g - g_hi.astype(F32)
    g_mid = r1.astype(BF16)
    g_lo = (r1 - g_mid.astype(F32)).astype(BF16)
    yield
    b = (jnp.dot(tri, g_hi, preferred_element_type=F32) + jnp.dot(tri, g_mid, preferred_element_type=F32)
         + jnp.dot(tri, g_lo, preferred_element_type=F32))
    b_ref[...] = b
    yield

    dn_nt = (((1,), (1,)), ((), ()))
    a_mat = None
    for lvl, w in enumerate(GLA_LEVELS):
        zero = jnp.zeros((w, GLA_DK), F32)
        qparts, kparts = [], []
        for mblk in range(c // (2 * w)):
            lo, mid, hi = mblk * 2 * w, mblk * 2 * w + w, mblk * 2 * w + 2 * w
            if reverse:
                r = b[mid:mid + 1, :]
                qparts += [q[lo:mid] * jnp.exp2(b[lo:mid] - r), zero]
                kparts += [zero, k[mid:hi] * jnp.exp2(r - b[mid:hi])]
            else:
                r = b[mid - 1:mid, :]
                qparts += [zero, q[mid:hi] * jnp.exp2(b[mid:hi] - r)]
                kparts += [k[lo:mid] * jnp.exp2(r - b[lo:mid]), zero]
        qn = jnp.concatenate(qparts, axis=0).astype(BF16)
        kn = jnp.concatenate(kparts, axis=0).astype(BF16)
        term = lax.dot_general(qn, kn, dn_nt, preferred_element_type=F32)
        if 2 * w < c:
            term = term * mask_ref[lvl]
        a_mat = term if a_mat is None else a_mat + term
        yield

    for i in range(c // GLA_SUB):
        r0 = i * GLA_SUB
        qi = q[r0:r0 + GLA_SUB]
        bi = b[r0:r0 + GLA_SUB]
        for s in range(0, GLA_SUB, 2):
            e0 = jnp.exp2(bi - b_ref[pl.ds(r0 + s, 1), :])
            e1 = jnp.exp2(bi - b_ref[pl.ds(r0 + s + 1, 1), :])
            w_ref[pl.ds((r0 + s) * GLA_SUB, 2 * GLA_SUB), :] = jnp.concatenate(
                [qi * e0, qi * e1], axis=0).astype(BF16)
        yield
    red = lax.dot_general(w_ref[...], kb, dn_nt, preferred_element_type=F32)
    yield
    lane = lax.broadcasted_iota(jnp.int32, (GLA_SUB, c), 1)
    rows = []
    for i in range(c // GLA_SUB):
        blk = jnp.zeros((GLA_SUB, c), F32)
        for s in range(GLA_SUB):
            col = i * GLA_SUB + s
            blk = jnp.where(lane == col, red[col * GLA_SUB:(col + 1) * GLA_SUB], blk)
        rows.append(blk)
        if i % 4 == 3:
            yield
    a_mat = a_mat + jnp.where(mask_ref[nl] > 0.5, jnp.concatenate(rows, axis=0), 0.0)

    st = st_ref[...]
    qe = (q * jnp.exp2(b)).astype(BF16)
    o = lax.dot_general(qe, st.astype(BF16), dn_nt, preferred_element_type=F32)
    yield
    o = o + jnp.dot(a_mat.astype(BF16), v, preferred_element_type=F32)
    o_ref[:, vsl] = o.astype(o_ref.dtype)
    yield

    r_end = 0 if reverse else c - 1
    b_end = b[r_end:r_end + 1, :]
    ke = (k * jnp.exp2(b_end - b)).astype(BF16)
    upd = lax.dot_general(v, ke, (((0,), (0,)), ((), ())), preferred_element_type=F32)
    yield
    st_ref[...] = st * jnp.exp2(b_end) + upd


def _gla_kernel(qf_ref, kf_ref, vf_ref, lrf_ref, qb_ref, kb_ref, vb_ref, lrb_ref,
                wgf_ref, bgf_ref, wgb_ref, bgb_ref, mask_ref, s0f_ref, s0b_ref,
                of_ref, ob_ref, sf_ref, sb_ref, stf_ref, stb_ref, wf_ref, wb_ref, bf_ref, bb_ref):
    cidx = pl.program_id(2)

    @pl.when(cidx == 0)
    def _():
        stf_ref[...] = s0f_ref[0]
        stb_ref[...] = s0b_ref[0]

    chains = []
    for hp in range(GLA_HEADS_PER_STEP):
        chains.append(_gla_chunk(qf_ref, kf_ref, vf_ref, lrf_ref, wgf_ref, bgf_ref, mask_ref.at[0],
                                 stf_ref.at[hp], wf_ref.at[hp], bf_ref.at[hp], of_ref, False, hp))
        chains.append(_gla_chunk(qb_ref, kb_ref, vb_ref, lrb_ref, wgb_ref, bgb_ref, mask_ref.at[1],
                                 stb_ref.at[hp], wb_ref.at[hp], bb_ref.at[hp], ob_ref, True, hp))
    while chains:
        chains = [ch for ch in chains if next(ch, True) is None]

    @pl.when(cidx == pl.num_programs(2) - 1)
    def _():
        sf_ref[0] = stf_ref[...]
        sb_ref[0] = stb_ref[...]


def _gla(p, lr, wgf, bgf, wgb, bgb, s0f, s0b, batch, seq):
    c = GLA_CHUNK
    nc = seq // c
    hps = GLA_HEADS_PER_STEP
    kw, vw = hps * GLA_DK, hps * GLA_DV
    kcol = P_KB // kw
    vcol = P_VB // vw

    def fwd(b, h, i):
        return b * nc + i

    def bwd(b, h, i):
        return b * nc + (nc - 1 - i)

    def data_specs(rowf):
        return [pl.BlockSpec((c, kw), lambda b, h, i: (rowf(b, h, i), h)),
                pl.BlockSpec((c, kw), lambda b, h, i: (rowf(b, h, i), kcol + h)),
                pl.BlockSpec((c, vw), lambda b, h, i: (rowf(b, h, i), vcol + h)),
                pl.BlockSpec((c, LANES), lambda b, h, i: (rowf(b, h, i), 0))]

    gate_specs = [pl.BlockSpec((LANES, kw), lambda b, h, i: (0, h)),
                  pl.BlockSpec((1, kw), lambda b, h, i: (0, h))]
    state_spec = pl.BlockSpec((1, hps, GLA_DV, GLA_DK), lambda b, h, i: (b, h, 0, 0))
    masks = jnp.asarray(_gla_masks())
    in_specs = (data_specs(fwd) + data_specs(bwd) + gate_specs + gate_specs
                + [pl.BlockSpec(masks.shape, lambda b, h, i: (0, 0, 0, 0)), state_spec, state_spec])
    out_specs = [pl.BlockSpec((c, vw), lambda b, h, i: (fwd(b, h, i), h)),
                 pl.BlockSpec((c, vw), lambda b, h, i: (bwd(b, h, i), h)),
                 state_spec, state_spec]
    out_shape = [jax.ShapeDtypeStruct((batch * seq, GLA_V_WIDTH), BF16)] * 2 + [
        jax.ShapeDtypeStruct((batch, GLA_HEADS, GLA_DV, GLA_DK), F32)] * 2
    return pl.pallas_call(
        _gla_kernel,
        grid=(batch, GLA_HEADS // hps, nc),
        in_specs=in_specs,
        out_specs=out_specs,
        out_shape=out_shape,
        scratch_shapes=[pltpu.VMEM((hps, GLA_DV, GLA_DK), F32), pltpu.VMEM((hps, GLA_DV, GLA_DK), F32),
                        pltpu.VMEM((hps, c * GLA_SUB, GLA_DK), BF16),
                        pltpu.VMEM((hps, c * GLA_SUB, GLA_DK), BF16),
                        pltpu.VMEM((hps, c, GLA_DK), F32), pltpu.VMEM((hps, c, GLA_DK), F32)],
        compiler_params=_params("parallel", "parallel", "arbitrary"),
        name="gla_scan",
    )(p, p, p, lr, p, p, p, lr, wgf, bgf, wgb, bgb, masks, s0f, s0b)


def _merge_kernel(oa_ref, of_ref, ob_ref, rb_ref, ga_ref, gb_ref, gn_ref, wa_ref, wg_ref, z_ref):
    og = of_ref[...].astype(F32) + ob_ref[...].astype(F32)
    parts = []
    for hh in range(GLA_HEADS):
        a = og[:, hh * GLA_DV:(hh + 1) * GLA_DV]
        ms = jnp.mean(a * a, axis=-1, keepdims=True)
        parts.append(a * lax.rsqrt(ms + EPS) * gn_ref[...])
    n = jnp.concatenate(parts, axis=1)
    rb = rb_ref[...].astype(F32)
    n = (n * (rb * _sigmoid(rb))).astype(BF16)
    y_gla = jnp.dot(n, wg_ref[...], preferred_element_type=F32)
    y_att = jnp.dot(oa_ref[...], wa_ref[...], preferred_element_type=F32)
    z = _sigmoid(ga_ref[...].astype(F32)) * y_att + _sigmoid(gb_ref[...].astype(F32)) * y_gla
    z_ref[...] = z.astype(z_ref.dtype)


def _merge(o_attn, o_f, o_b, p, gla_norm, w_attn_o, w_gla_o, tm=256):
    m, d = o_attn.shape
    row = lambda i: (i, 0)
    const = lambda i: (0, 0)
    wspec = pl.BlockSpec((d, d), const, pipeline_mode=pl.Buffered(1))
    return pl.pallas_call(
        _merge_kernel,
        grid=(m // tm,),
        in_specs=[pl.BlockSpec((tm, d), row), pl.BlockSpec((tm, d), row), pl.BlockSpec((tm, d), row),
                  pl.BlockSpec((tm, d), lambda i: (i, P_RB // d)),
                  pl.BlockSpec((tm, d), lambda i: (i, P_GA // d)),
                  pl.BlockSpec((tm, d), lambda i: (i, P_GB // d)),
                  pl.BlockSpec((1, GLA_DV), const), wspec, wspec],
        out_specs=pl.BlockSpec((tm, d), row),
        out_shape=jax.ShapeDtypeStruct((m, d), BF16),
        compiler_params=_params("parallel"),
        name="merge_gates",
    )(o_attn, o_f, o_b, p, p, p, gla_norm.reshape(1, GLA_DV), w_attn_o, w_gla_o)


def _out_proj_kernel(z_ref, w_ref, x_ref, gate_ref, o_ref):
    y = jnp.dot(z_ref[...], w_ref[...], preferred_element_type=F32)
    o_ref[...] = x_ref[...] + gate_ref[0] * y


def _out_proj(z, w_out, x2, gate, rows_per_batch, tm=512):
    m, d = x2.shape
    tm = min(tm, rows_per_batch)
    tpb = rows_per_batch // tm
    return pl.pallas_call(
        _out_proj_kernel,
        grid=(m // tm,),
        in_specs=[pl.BlockSpec((tm, d), lambda i: (i, 0)),
                  pl.BlockSpec((d, d), lambda i: (0, 0), pipeline_mode=pl.Buffered(1)),
                  pl.BlockSpec((tm, d), lambda i: (i, 0)),
                  pl.BlockSpec((1, 1, d), lambda i: (i // tpb, 0, 0))],
        out_specs=pl.BlockSpec((tm, d), lambda i: (i, 0)),
        out_shape=jax.ShapeDtypeStruct((m, d), F32),
        compiler_params=_params("parallel"),
        name="out_proj_residual",
    )(z, w_out, x2, gate)


FFN_SUB = 256
FFN_SUBS = 2
FFN_HALO = 16
FFN_OUT_COLS = 512


def _ffn_kernel(x_ref, xp_ref, xn_ref, shift_ref, scale_ref, gate_ref, gain_ref, wu_ref, cw_ref, cb_ref,
                wd_ref, o_ref, h_ref, *, tpb):
    i = pl.program_id(0)
    kk = pl.program_id(1)
    tm = x_ref.shape[0]
    d = x_ref.shape[1]
    ext = tm + 2 * FFN_HALO

    def modnorm(x):
        ms = jnp.mean(x * x, axis=-1, keepdims=True)
        y = x * lax.rsqrt(ms + EPS) * gain_ref[...]
        return y * (1.0 + scale_ref[0]) + shift_ref[0]

    @pl.when(kk == 0)
    def _():
        keep_prev = jnp.where((i % tpb) == 0, 0.0, 1.0)
        keep_next = jnp.where((i % tpb) == tpb - 1, 0.0, 1.0)
        h_ref[0:FFN_HALO, :] = (modnorm(xp_ref[...]) * keep_prev).astype(BF16)
        h_ref[FFN_HALO:FFN_HALO + tm, :] = modnorm(x_ref[...]).astype(BF16)
        h_ref[FFN_HALO + tm:ext, :] = (modnorm(xn_ref[...]) * keep_next).astype(BF16)
        o_ref[...] = jnp.zeros_like(o_ref)

    w2 = 2 * FFN_SUB
    us = [jnp.dot(h_ref[...], wu_ref[:, s * w2:(s + 1) * w2], preferred_element_type=F32)
          for s in range(FFN_SUBS)]
    for s in range(FFN_SUBS):
        u = us[s]
        cols = slice(s * w2, (s + 1) * w2)
        rows = slice(FFN_HALO, FFN_HALO + tm)
        conv = (cw_ref[0:1, cols] * pltpu.roll(u, 1, axis=0)[rows]
                + cw_ref[1:2, cols] * u[rows]
                + cw_ref[2:3, cols] * pltpu.roll(u, ext - 1, axis=0)[rows] + cb_ref[:, cols])
        a = conv[:, :FFN_SUB]
        act = (a * _sigmoid(a) * conv[:, FFN_SUB:]).astype(BF16)
        for nb in range(d // FFN_OUT_COLS):
            oc = slice(nb * FFN_OUT_COLS, (nb + 1) * FFN_OUT_COLS)
            o_ref[:, oc] += jnp.dot(act, wd_ref[s * FFN_SUB:(s + 1) * FFN_SUB, oc],
                                    preferred_element_type=F32)

    @pl.when(kk == pl.num_programs(1) - 1)
    def _():
        o_ref[...] = x_ref[...] + gate_ref[0] * o_ref[...]


def _conv_ffn(x2, shift, scale, gate, gain, w_up_r, conv_w_r, conv_b_r, w_down, rows_per_batch, tm=1024):
    m, d = x2.shape
    dff = w_down.shape[0]
    tm = min(tm, rows_per_batch)
    tpb = rows_per_batch // tm
    hb = tm // FFN_HALO
    nhalo = m // FFN_HALO
    tk = FFN_SUB * FFN_SUBS
    nk = dff // tk
    once = pl.Buffered(1)
    return pl.pallas_call(
        functools.partial(_ffn_kernel, tpb=tpb),
        grid=(m // tm, nk),
        in_specs=[pl.BlockSpec((tm, d), lambda i, k: (i, 0), pipeline_mode=once),
                  pl.BlockSpec((FFN_HALO, d), lambda i, k: (jnp.maximum(i * hb - 1, 0), 0)),
                  pl.BlockSpec((FFN_HALO, d), lambda i, k: (jnp.minimum((i + 1) * hb, nhalo - 1), 0)),
                  pl.BlockSpec((1, 1, d), lambda i, k: (i // tpb, 0, 0)),
                  pl.BlockSpec((1, 1, d), lambda i, k: (i // tpb, 0, 0)),
                  pl.BlockSpec((1, 1, d), lambda i, k: (i // tpb, 0, 0)),
                  pl.BlockSpec((1, d), lambda i, k: (0, 0)),
                  pl.BlockSpec((d, 2 * tk), lambda i, k: (0, k)),
                  pl.BlockSpec((3, 2 * tk), lambda i, k: (0, k)),
                  pl.BlockSpec((1, 2 * tk), lambda i, k: (0, k)),
                  pl.BlockSpec((tk, d), lambda i, k: (k, 0))],
        out_specs=pl.BlockSpec((tm, d), lambda i, k: (i, 0)),
        out_shape=jax.ShapeDtypeStruct((m, d), F32),
        scratch_shapes=[pltpu.VMEM((tm + 2 * FFN_HALO, d), BF16)],
        compiler_params=_params("parallel", "arbitrary"),
        name="conv_ffn",
    )(x2, x2, x2, shift, scale, gate, gain.reshape(1, d), w_up_r, conv_w_r, conv_b_r, w_down)


def _regroup_ffn_cols(a):
    lead = a.shape[:-1]
    nblk = D_FF // FFN_SUB
    a = a.reshape(*lead, 2, nblk, FFN_SUB)
    return jnp.swapaxes(a, -3, -2).reshape(*lead, 2 * D_FF)


def _rope_tables(n):
    rows = n // GRID_W
    row = jnp.repeat(jnp.arange(rows), GRID_W)
    col = jnp.tile(jnp.arange(GRID_W), rows)
    n_freq = HEAD_DIM // 4
    inv = ROPE_THETA ** (-jnp.arange(n_freq, dtype=F32) / n_freq)
    ang = jnp.concatenate([row[:, None] * inv, col[:, None] * inv], axis=-1)
    cos, sin = jnp.cos(ang), jnp.sin(ang)
    return jnp.concatenate([cos, cos], axis=-1), jnp.concatenate([-sin, sin], axis=-1)


def _split_w_in(w_in):
    names = ("qa", "ka", "va", "qb", "kb", "vb", "rb", "lrf", "lrb", "ga", "gb")
    splits = (ATTN_WIDTH, KV_WIDTH, KV_WIDTH, GLA_K_WIDTH, GLA_K_WIDTH, GLA_V_WIDTH, GLA_V_WIDTH,
              GLA_LOWRANK, GLA_LOWRANK, D_MODEL, D_MODEL)
    start = dict(zip(names, np.cumsum((0,) + splits[:-1]).tolist()))
    width = dict(zip(names, splits))
    bw = STAGE_COLS
    wt = jnp.swapaxes(w_in, 0, 1)

    def starts(group):
        out = []
        for nm in group:
            assert start[nm] % 8 == 0 and width[nm] % bw == 0
            out += [start[nm] + t * bw for t in range(width[nm] // bw)]
        return out

    w_qk = _transpose_cast_rows(wt, starts(("qa", "ka")), bw)
    w_p = _transpose_cast_rows(wt, starts(("qb", "kb", "vb", "rb", "ga", "gb", "va")), bw)
    lr0, nlr = start["lrf"], 2 * GLA_LOWRANK
    pad = jnp.zeros((w_in.shape[0], LANES - nlr), w_in.dtype)
    w_lr = jnp.concatenate([jnp.swapaxes(wt[lr0:lr0 + nlr], 0, 1), pad], axis=-1).astype(BF16)
    return w_qk, w_p, w_lr


def kernel(x, c, ctx, c_ctx, w_mod, b_mod, g_mix, w_in, q_norm, k_norm, attn_sink, w_gate_f, b_gate_f,
           w_gate_b, b_gate_b, gla_norm, w_attn_o, w_gla_o, w_out, g_ffn, w_up, conv_w, conv_b, w_down):
    batch, seq, d = x.shape
    ctx_len = ctx.shape[1]
    assert w_mod.shape[0] == 1, "single-layer kernel"
    assert d == D_MODEL and seq % GLA_CHUNK == 0 and ctx_len % GLA_CHUNK == 0 and batch <= 7

    cc = jnp.zeros((8, d), F32).at[:batch].set(c).at[batch].set(c_ctx)
    mod = _modulation(cc, w_mod[0], b_mod[0]).reshape(8, 6, d)
    mod_x = [mod[:batch, jj][:, None, :] for jj in range(6)]
    mod_c = [mod[batch:batch + 1, jj][:, None, :] for jj in range(6)]

    w_qk, w_p, w_lr = _split_w_in(w_in[0])
    q_fold = HEAD_DIM ** -0.5 * LOG2E
    qk_norm_w = jnp.concatenate([jnp.tile(q_norm[0] * q_fold, N_Q_HEADS), jnp.tile(k_norm[0], N_KV_HEADS)])
    p_scale = jnp.ones((P_WIDTH,), F32).at[P_QB:P_QB + GLA_K_WIDTH].set(GLA_DK ** -0.5)
    cos2, sin2 = _rope_tables(seq)

    x2 = x.reshape(batch * seq, d)
    c2 = ctx.reshape(batch * ctx_len, d)
    qk, lr = _norm_matmul(x2, mod_x[0], mod_x[1], g_mix[0], w_qk, qk_norm_w, mode="qk",
                          rows_per_batch=seq, cos2=cos2, sin2=sin2, w_lr=w_lr, tm=512, tn=QK_WIDTH)
    p = _norm_matmul(x2, mod_x[0], mod_x[1], g_mix[0], w_p, p_scale, mode="plain", rows_per_batch=seq,
                     tn=1536)
    qk_c, lr_c = _norm_matmul(c2, mod_c[0], mod_c[1], g_mix[0], w_qk, qk_norm_w, mode="qk",
                              rows_per_batch=batch * ctx_len, w_lr=w_lr, tm=512, tn=QK_WIDTH)
    p_c = _norm_matmul(c2, mod_c[0], mod_c[1], g_mix[0], w_p, p_scale, mode="plain",
                       rows_per_batch=batch * ctx_len)

    o_attn = _attention(qk, p, qk_c, p_c, attn_sink[0], batch, seq, ctx_len)

    wgf = jnp.zeros((LANES, GLA_K_WIDTH), F32).at[:GLA_LOWRANK].set(w_gate_f[0])
    wgb = jnp.zeros((LANES, GLA_K_WIDTH), F32).at[GLA_LOWRANK:2 * GLA_LOWRANK].set(w_gate_b[0])
    bgf = b_gate_f[0].reshape(1, GLA_K_WIDTH)
    bgb = b_gate_b[0].reshape(1, GLA_K_WIDTH)
    zero_state = jnp.zeros((batch, GLA_HEADS, GLA_DV, GLA_DK), F32)
    _, _, sf, sb = _gla(p_c, lr_c, wgf, bgf, wgb, bgb, zero_state, zero_state, batch, ctx_len)
    o_f, o_b, _, _ = _gla(p, lr, wgf, bgf, wgb, bgb, sf, sb, batch, seq)

    z = _merge(o_attn, o_f, o_b, p, gla_norm[0], w_attn_o[0].astype(BF16), w_gla_o[0].astype(BF16))
    x1 = _out_proj(z, w_out[0].astype(BF16), x2, mod_x[2], seq)

    nblk = D_FF // FFN_SUB
    w_up_r = _permute_cast_cols(w_up[0], [(j % 2) * nblk + j // 2 for j in range(2 * nblk)], FFN_SUB)
    out = _conv_ffn(x1, mod_x[3], mod_x[4], mod_x[5], g_ffn[0], w_up_r,
                    _regroup_ffn_cols(conv_w[0]), _regroup_ffn_cols(conv_b[0]).reshape(1, 2 * D_FF),
                    w_down[0].astype(BF16), seq)
    return out.reshape(batch, seq, d)
```

```python
import functools

import numpy as np
import jax
import jax.numpy as jnp
from jax import lax
from jax.experimental import pallas as pl
from jax.experimental.pallas import tpu as pltpu

F32 = jnp.float32
BF16 = jnp.bfloat16

D_MODEL = 2048
GRID_W = 64
HEAD_DIM = 128
N_Q_HEADS = 16
N_KV_HEADS = 4
Q_PER_KV = N_Q_HEADS // N_KV_HEADS
WINDOW = 128
ROPE_THETA = 10000.0
GLA_HEADS = 4
GLA_DK = D_MODEL // 2 // GLA_HEADS
GLA_DV = D_MODEL // GLA_HEADS
GLA_LOWRANK = 16
GLA_GATE_NORM = 16.0
D_FF = 5632
EPS = 1e-6
ATTN_WIDTH = N_Q_HEADS * HEAD_DIM
KV_WIDTH = N_KV_HEADS * HEAD_DIM
GLA_K_WIDTH = GLA_HEADS * GLA_DK
GLA_V_WIDTH = GLA_HEADS * GLA_DV

LANES = 128
MXU_WIDTH = 256
VMEM_LIMIT_BYTES = 60 * 1024 * 1024

GLA_CHUNK = 128
GLA_SUB = 8
GLA_LEVELS = (64, 32, 16, 8)
LOG2E = 1.4426950408889634
GLA_HEADS_PER_STEP = 4

P_QB, P_KB, P_VB, P_RB, P_GA, P_GB, P_VA = 0, 1024, 2048, 4096, 6144, 8192, 10240
P_WIDTH = 10752
QK_WIDTH = ATTN_WIDTH + KV_WIDTH
QK_GROUP = 512
STAGE_COLS = 512


def _params(*sem):
    return pltpu.CompilerParams(dimension_semantics=sem, vmem_limit_bytes=VMEM_LIMIT_BYTES)


def _sigmoid(x):
    return 1.0 / (1.0 + jnp.exp(-x))


def _permute_cast_kernel(src_ref, a_ref, o_ref):
    o_ref[...] = a_ref[...].astype(o_ref.dtype)


def _permute_cast_cols(w, src_blocks, bw):
    kdim = w.shape[0]
    n = len(src_blocks)
    src = jnp.asarray(np.asarray(src_blocks, np.int32))
    return pl.pallas_call(
        _permute_cast_kernel,
        grid_spec=pltpu.PrefetchScalarGridSpec(
            num_scalar_prefetch=1,
            grid=(n,),
            in_specs=[pl.BlockSpec((kdim, bw), lambda j, s: (0, s[j]))],
            out_specs=pl.BlockSpec((kdim, bw), lambda j, s: (0, j)),
        ),
        out_shape=jax.ShapeDtypeStruct((kdim, n * bw), BF16),
        compiler_params=_params("arbitrary"),
        name="permute_cast",
    )(src, w)


def _transpose_cast_kernel(start_ref, a_ref, o_ref):
    o_ref[...] = a_ref[...].T.astype(o_ref.dtype)


def _transpose_cast_rows(wt, row_starts, bw):
    kdim = wt.shape[1]
    n = len(row_starts)
    sub = 8
    assert all(r % sub == 0 for r in row_starts)
    starts = jnp.asarray(np.asarray(row_starts, np.int32) // sub)
    return pl.pallas_call(
        _transpose_cast_kernel,
        grid_spec=pltpu.PrefetchScalarGridSpec(
            num_scalar_prefetch=1,
            grid=(n,),
            in_specs=[pl.BlockSpec((pl.Element(bw), pl.Element(kdim)), lambda j, s: (s[j] * sub, 0))],
            out_specs=pl.BlockSpec((kdim, bw), lambda j, s: (0, j)),
        ),
        out_shape=jax.ShapeDtypeStruct((kdim, n * bw), BF16),
        compiler_params=_params("arbitrary"),
        name="transpose_cast",
    )(starts, wt)


def _mod_kernel(a_ref, w_ref, b_ref, o_ref):
    a = a_ref[...]
    s = (a * _sigmoid(a)).astype(BF16)
    o_ref[...] = jnp.dot(s, w_ref[...].astype(BF16), preferred_element_type=F32) + b_ref[...]


def _modulation(cc, w_mod, b_mod):
    d, n = w_mod.shape
    tn = 1024
    return pl.pallas_call(
        _mod_kernel,
        grid=(n // tn,),
        in_specs=[pl.BlockSpec((8, d), lambda j: (0, 0)),
                  pl.BlockSpec((d, tn), lambda j: (0, j)),
                  pl.BlockSpec((1, tn), lambda j: (0, j))],
        out_specs=pl.BlockSpec((8, tn), lambda j: (0, j)),
        out_shape=jax.ShapeDtypeStruct((8, n), F32),
        compiler_params=_params("parallel"),
        name="modulation",
    )(cc, w_mod, b_mod.reshape(1, n))


def _norm_mm_kernel(*refs, mode, rope, has_lr, tn):
    it = iter(refs)
    x_ref, shift_ref, scale_ref, gain_ref, w_ref, cs_ref = [next(it) for _ in range(6)]
    cos_ref = sin_ref = wlr_ref = lr_ref = seg_ref = perm_ref = None
    if mode == "qk":
        seg_ref, perm_ref = next(it), next(it)
    if rope:
        cos_ref, sin_ref = next(it), next(it)
    if has_lr:
        wlr_ref = next(it)
    o_ref = next(it)
    if has_lr:
        lr_ref = next(it)
    h_ref = next(it)

    @pl.when(pl.program_id(1) == 0)
    def _():
        x = x_ref[...]
        ms = jnp.mean(x * x, axis=-1, keepdims=True)
        y = x * lax.rsqrt(ms + EPS) * gain_ref[...]
        h = (y * (1.0 + scale_ref[0]) + shift_ref[0]).astype(BF16)
        h_ref[...] = h
        if has_lr:
            lr_ref[...] = jnp.dot(h, wlr_ref[...], preferred_element_type=F32)

    if mode == "plain":
        acc = jnp.dot(h_ref[...], w_ref[...], preferred_element_type=F32)
        o_ref[...] = (acc * cs_ref[...]).astype(o_ref.dtype)
    else:
        for gb in range(tn // QK_GROUP):
            big = jnp.dot(h_ref[...], w_ref[:, gb * QK_GROUP:(gb + 1) * QK_GROUP],
                          preferred_element_type=F32)
            for cb in range(QK_GROUP // MXU_WIDTH):
                c0 = gb * QK_GROUP + cb * MXU_WIDTH
                acc = big[:, cb * MXU_WIDTH:(cb + 1) * MXU_WIDTH]
                ss = jnp.dot((acc * acc).astype(BF16), seg_ref[...], preferred_element_type=F32)
                an = acc * lax.rsqrt(ss * (1.0 / HEAD_DIM) + EPS) * cs_ref[:, c0:c0 + MXU_WIDTH]
                if rope:
                    rot = jnp.dot(an.astype(BF16), perm_ref[...], preferred_element_type=F32)
                for hh in range(MXU_WIDTH // HEAD_DIM):
                    hs = slice(hh * HEAD_DIM, (hh + 1) * HEAD_DIM)
                    a = an[:, hs]
                    if rope:
                        a = a * cos_ref[...] + rot[:, hs] * sin_ref[...]
                    o_ref[:, c0 + hh * HEAD_DIM:c0 + (hh + 1) * HEAD_DIM] = a.astype(o_ref.dtype)


def _norm_matmul(x2, shift, scale, gain, w, colvec, *, mode, rows_per_batch, cos2=None, sin2=None,
                 w_lr=None, tm=1024, tn=512):
    m, d = x2.shape
    n = w.shape[1]
    tm = min(tm, rows_per_batch)
    tpb = rows_per_batch // tm
    rope = cos2 is not None
    has_lr = w_lr is not None
    in_specs = [pl.BlockSpec((tm, d), lambda i, j: (i, 0)),
                pl.BlockSpec((1, 1, d), lambda i, j: (i // tpb, 0, 0)),
                pl.BlockSpec((1, 1, d), lambda i, j: (i // tpb, 0, 0)),
                pl.BlockSpec((1, d), lambda i, j: (0, 0)),
                pl.BlockSpec((d, tn), lambda i, j: (0, j)),
                pl.BlockSpec((1, tn), lambda i, j: (0, j))]
    args = [x2, shift, scale, gain.reshape(1, d), w, colvec.reshape(1, n)]
    if mode == "qk":
        lane = np.arange(MXU_WIDTH)
        same_head = lane[:, None] // HEAD_DIM == lane[None, :] // HEAD_DIM
        rolled = (lane[:, None] % HEAD_DIM) == ((lane[None, :] - HEAD_DIM // 2) % HEAD_DIM)
        in_specs += [pl.BlockSpec((MXU_WIDTH, MXU_WIDTH), lambda i, j: (0, 0))] * 2
        args += [jnp.asarray(same_head, BF16), jnp.asarray(same_head & rolled, BF16)]
    if rope:
        in_specs += [pl.BlockSpec((tm, HEAD_DIM), lambda i, j: (i % tpb, 0))] * 2
        args += [cos2, sin2]
    out_shape = [jax.ShapeDtypeStruct((m, n), BF16)]
    out_specs = [pl.BlockSpec((tm, tn), lambda i, j: (i, j))]
    if has_lr:
        in_specs.append(pl.BlockSpec((d, LANES), lambda i, j: (0, 0)))
        args.append(w_lr)
        out_shape.append(jax.ShapeDtypeStruct((m, LANES), F32))
        out_specs.append(pl.BlockSpec((tm, LANES), lambda i, j: (i, 0)))
    res = pl.pallas_call(
        functools.partial(_norm_mm_kernel, mode=mode, rope=rope, has_lr=has_lr, tn=tn),
        grid=(m // tm, n // tn),
        in_specs=in_specs,
        out_specs=out_specs,
        out_shape=out_shape,
        scratch_shapes=[pltpu.VMEM((tm, d), BF16)],
        compiler_params=_params("parallel", "arbitrary"),
        name="norm_matmul_" + mode,
    )(*args)
    return res if has_lr else res[0]


ATTN_Q_BLOCKS = 2


def _attn_kernel(sink_ref, q_ref, *refs, nblk):
    nq = ATTN_Q_BLOCKS
    k_refs, kx_ref = refs[:nq + 2], refs[nq + 2]
    v_refs, vx_ref = refs[nq + 3:2 * nq + 5], refs[2 * nq + 5]
    o_ref = refs[2 * nq + 6]
    i = pl.program_id(1)
    blk = HEAD_DIM
    dn = (((1,), (1,)), ((), ()))
    row = lax.broadcasted_iota(jnp.int32, (blk, blk), 0)
    col = lax.broadcasted_iota(jnp.int32, (blk, blk), 1)
    ninf = jnp.float32(-jnp.inf)
    tri_p = jnp.concatenate([jnp.where(col >= row, 0.0, ninf)] * Q_PER_KV, axis=0)
    tri_n = jnp.concatenate([jnp.where(col <= row, 0.0, ninf)] * Q_PER_KV, axis=0)
    n_ctx = kx_ref.shape[0] // blk

    for a in range(nq):
        gblk = i * nq + a
        bias_p = tri_p + jnp.where(gblk > 0, 0.0, ninf)
        bias_n = tri_n + jnp.where(gblk < nblk - 1, 0.0, ninf)
        rows = slice(a * blk, (a + 1) * blk)
        for h in range(N_KV_HEADS):
            hs = slice(h * HEAD_DIM, (h + 1) * HEAD_DIM)
            qs = jnp.concatenate(
                [q_ref[rows, (h * Q_PER_KV + g) * HEAD_DIM:(h * Q_PER_KV + g + 1) * HEAD_DIM]
                 for g in range(Q_PER_KV)], axis=0)

            def scores(k_ref):
                return lax.dot_general(qs, k_ref[:, hs], dn, preferred_element_type=F32)

            s_x = scores(kx_ref)
            pieces = [scores(k_refs[a]) + bias_p, scores(k_refs[a + 1]), scores(k_refs[a + 2]) + bias_n]
            pieces += [s_x[:, j * blk:(j + 1) * blk] for j in range(n_ctx)]
            sink = jnp.concatenate(
                [jnp.full((blk, 1), sink_ref[h * Q_PER_KV + g] * LOG2E, F32) for g in range(Q_PER_KV)],
                axis=0)
            mx = pieces[0]
            for s in pieces[1:]:
                mx = jnp.maximum(mx, s)
            m = jnp.maximum(jnp.max(mx, axis=-1, keepdims=True), sink)
            probs = [jnp.exp2(s - m) for s in pieces]
            psum = probs[0]
            for pr in probs[1:]:
                psum = psum + pr
            denom = jnp.exp2(sink - m) + jnp.sum(psum, axis=-1, keepdims=True)
            p_x = jnp.concatenate(probs[3:], axis=1) if n_ctx > 1 else probs[3]
            o = (jnp.dot(probs[0].astype(BF16), v_refs[a][:, hs], preferred_element_type=F32)
                 + jnp.dot(probs[1].astype(BF16), v_refs[a + 1][:, hs], preferred_element_type=F32)
                 + jnp.dot(probs[2].astype(BF16), v_refs[a + 2][:, hs], preferred_element_type=F32)
                 + jnp.dot(p_x.astype(BF16), vx_ref[:, hs], preferred_element_type=F32))
            o = o / denom
            for g in range(Q_PER_KV):
                c0 = (h * Q_PER_KV + g) * HEAD_DIM
                o_ref[rows, c0:c0 + HEAD_DIM] = o[g * blk:(g + 1) * blk].astype(o_ref.dtype)


def _attention(qk, p, qk_c, p_c, sink, batch, seq, ctx_len):
    blk = HEAD_DIM
    nblk = seq // blk
    nq = ATTN_Q_BLOCKS
    nstep = nblk // nq
    kcol = ATTN_WIDTH // KV_WIDTH
    vcol = P_VA // KV_WIDTH

    def kv_specs(col0):
        def spec(shift):
            return pl.BlockSpec(
                (blk, KV_WIDTH), lambda b, i: (b * nblk + jnp.clip(i * nq + shift, 0, nblk - 1), col0))
        return [spec(shift) for shift in range(-1, nq + 1)]

    in_specs = ([pl.BlockSpec(memory_space=pltpu.SMEM),
                 pl.BlockSpec((nq * blk, ATTN_WIDTH), lambda b, i: (b * nstep + i, 0))]
                + kv_specs(kcol) + [pl.BlockSpec((ctx_len, KV_WIDTH), lambda b, i: (b, kcol))]
                + kv_specs(vcol) + [pl.BlockSpec((ctx_len, KV_WIDTH), lambda b, i: (b, vcol))])
    return pl.pallas_call(
        functools.partial(_attn_kernel, nblk=nblk),
        grid=(batch, nstep),
        in_specs=in_specs,
        out_specs=pl.BlockSpec((nq * blk, ATTN_WIDTH), lambda b, i: (b * nstep + i, 0)),
        out_shape=jax.ShapeDtypeStruct((batch * seq, ATTN_WIDTH), BF16),
        compiler_params=_params("parallel", "arbitrary"),
        name="window_attention",
    )(sink, qk, *([qk] * (nq + 2)), qk_c, *([p] * (nq + 2)), p_c)


def _gla_masks():
    c = GLA_CHUNK
    t = np.arange(c)[:, None]
    s = np.arange(c)[None, :]
    fwd = []
    for w in GLA_LEVELS:
        fwd.append((t // (2 * w) == s // (2 * w)) & (t % (2 * w) >= w) & (s % (2 * w) < w))
    fwd.append((t // GLA_SUB == s // GLA_SUB) & (s <= t))
    fwd.append(s <= t)
    fwd = np.stack(fwd).astype(np.float32)
    return np.stack([fwd, np.transpose(fwd, (0, 2, 1))])


def _gla_chunk(q_ref, k_ref, v_ref, lr_ref, wg_ref, bg_ref, mask_ref, st_ref, w_ref, b_ref, o_ref, reverse, hp):
    c = GLA_CHUNK
    nl = len(GLA_LEVELS)
    ksl = slice(hp * GLA_DK, (hp + 1) * GLA_DK)
    vsl = slice(hp * GLA_DV, (hp + 1) * GLA_DV)
    q = q_ref[:, ksl].astype(F32)
    kb = k_ref[:, ksl]
    k = kb.astype(F32)
    v = v_ref[:, vsl]
    xg = jnp.dot(lr_ref[...], wg_ref[:, ksl], preferred_element_type=F32,
                 precision=lax.Precision.HIGHEST) + bg_ref[:, ksl]
    yield
    g = (jnp.minimum(xg, 0.0) - jnp.log1p(jnp.exp(-jnp.abs(xg)))) * (LOG2E / GLA_GATE_NORM)

    tri = mask_ref[nl + 1].astype(BF16)
    g_hi = g.astype(BF16)
    r1 = g - g_hi.astype(F32)
    g_mid = r1.astype(BF16)
    g_lo = (r1 - g_mid.astype(F32)).astype(BF16)
    yield
    b = (jnp.dot(tri, g_hi, preferred_element_type=F32) + jnp.dot(tri, g_mid, preferred_element_type=F32)
         + jnp.dot(tri, g_lo, preferred_element_type=F32))
    b_ref[...] = b
    yield

    dn_nt = (((1,), (1,)), ((), ()))
    a_mat = None
    for lvl, w in enumerate(GLA_LEVELS):
        zero = jnp.zeros((w, GLA_DK), F32)
        qparts, kparts = [], []
        for mblk in range(c // (2 * w)):
            lo, mid, hi = mblk * 2 * w, mblk * 2 * w + w, mblk * 2 * w + 2 * w
            if reverse:
                r = b[mid:mid + 1, :]
                qparts += [q[lo:mid] * jnp.exp2(b[lo:mid] - r), zero]
                kparts += [zero, k[mid:hi] * jnp.exp2(r - b[mid:hi])]
            else:
                r = b[mid - 1:mid, :]
                qparts += [zero, q[mid:hi] * jnp.exp2(b[mid:hi] - r)]
                kparts += [k[lo:mid] * jnp.exp2(r - b[lo:mid]), zero]
        qn = jnp.concatenate(qparts, axis=0).astype(BF16)
        kn = jnp.concatenate(kparts, axis=0).astype(BF16)
        term = lax.dot_general(qn, kn, dn_nt, preferred_element_type=F32)
        if 2 * w < c:
            term = term * mask_ref[lvl]
        a_mat = term if a_mat is None else a_mat + term
        yield

    for i in range(c // GLA_SUB):
        r0 = i * GLA_SUB
        qi = q[r0:r0 + GLA_SUB]
        bi = b[r0:r0 + GLA_SUB]
        for s in range(0, GLA_SUB, 2):
            e0 = jnp.exp2(bi - b_ref[pl.ds(r0 + s, 1), :])
            e1 = jnp.exp2(bi - b_ref[pl.ds(r0 + s + 1, 1), :])
            w_ref[pl.ds((r0 + s) * GLA_SUB, 2 * GLA_SUB), :] = jnp.concatenate(
                [qi * e0, qi * e1], axis=0).astype(BF16)
        yield
    red = lax.dot_general(w_ref[...], kb, dn_nt, preferred_element_type=F32)
    yield
    lane = lax.broadcasted_iota(jnp.int32, (GLA_SUB, c), 1)
    rows = []
    for i in range(c // GLA_SUB):
        blk = jnp.zeros((GLA_SUB, c), F32)
        for s in range(GLA_SUB):
            col = i * GLA_SUB + s
            blk = jnp.where(lane == col, red[col * GLA_SUB:(col + 1) * GLA_SUB], blk)
        rows.append(blk)
        if i % 4 == 3:
            yield
    a_mat = a_mat + jnp.where(mask_ref[nl] > 0.5, jnp.concatenate(rows, axis=0), 0.0)

    st = st_ref[...]
    qe = (q * jnp.exp2(b)).astype(BF16)
    o = lax.dot_general(qe, st.astype(BF16), dn_nt, preferred_element_type=F32)
    yield
    o = o + jnp.dot(a_mat.astype(BF16), v, preferred_element_type=F32)
    o_ref[:, vsl] = o.astype(o_ref.dtype)
    yield

    r_end = 0 if reverse else c - 1
    b_end = b[r_end:r_end + 1, :]
    ke = (k * jnp.exp2(b_end - b)).astype(BF16)
    upd = lax.dot_general(v, ke, (((0,), (0,)), ((), ())), preferred_element_type=F32)
    yield
    st_ref[...] = st * jnp.exp2(b_end) + upd


def _gla_kernel(qf_ref, kf_ref, vf_ref, lrf_ref, qb_ref, kb_ref, vb_ref, lrb_ref,
                wgf_ref, bgf_ref, wgb_ref, bgb_ref, mask_ref, s0f_ref, s0b_ref,
                of_ref, ob_ref, sf_ref, sb_ref, stf_ref, stb_ref, wf_ref, wb_ref, bf_ref, bb_ref):
    cidx = pl.program_id(2)

    @pl.when(cidx == 0)
    def _():
        stf_ref[...] = s0f_ref[0]
        stb_ref[...] = s0b_ref[0]

    chains = []
    for hp in range(GLA_HEADS_PER_STEP):
        chains.append(_gla_chunk(qf_ref, kf_ref, vf_ref, lrf_ref, wgf_ref, bgf_ref, mask_ref.at[0],
                                 stf_ref.at[hp], wf_ref.at[hp], bf_ref.at[hp], of_ref, False, hp))
        chains.append(_gla_chunk(qb_ref, kb_ref, vb_ref, lrb_ref, wgb_ref, bgb_ref, mask_ref.at[1],
                                 stb_ref.at[hp], wb_ref.at[hp], bb_ref.at[hp], ob_ref, True, hp))
    while chains:
        chains = [ch for ch in chains if next(ch, True) is None]

    @pl.when(cidx == pl.num_programs(2) - 1)
    def _():
        sf_ref[0] = stf_ref[...]
        sb_ref[0] = stb_ref[...]


def _gla(p, lr, wgf, bgf, wgb, bgb, s0f, s0b, batch, seq):
    c = GLA_CHUNK
    nc = seq // c
    hps = GLA_HEADS_PER_STEP
    kw, vw = hps * GLA_DK, hps * GLA_DV
    kcol = P_KB // kw
    vcol = P_VB // vw

    def fwd(b, h, i):
        return b * nc + i

    def bwd(b, h, i):
        return b * nc + (nc - 1 - i)

    def data_specs(rowf):
        return [pl.BlockSpec((c, kw), lambda b, h, i: (rowf(b, h, i), h)),
                pl.BlockSpec((c, kw), lambda b, h, i: (rowf(b, h, i), kcol + h)),
                pl.BlockSpec((c, vw), lambda b, h, i: (rowf(b, h, i), vcol + h)),
                pl.BlockSpec((c, LANES), lambda b, h, i: (rowf(b, h, i), 0))]

    gate_specs = [pl.BlockSpec((LANES, kw), lambda b, h, i: (0, h)),
                  pl.BlockSpec((1, kw), lambda b, h, i: (0, h))]
    state_spec = pl.BlockSpec((1, hps, GLA_DV, GLA_DK), lambda b, h, i: (b, h, 0, 0))
    masks = jnp.asarray(_gla_masks())
    in_specs = (data_specs(fwd) + data_specs(bwd) + gate_specs + gate_specs
                + [pl.BlockSpec(masks.shape, lambda b, h, i: (0, 0, 0, 0)), state_spec, state_spec])
    out_specs = [pl.BlockSpec((c, vw), lambda b, h, i: (fwd(b, h, i), h)),
                 pl.BlockSpec((c, vw), lambda b, h, i: (bwd(b, h, i), h)),
                 state_spec, state_spec]
    out_shape = [jax.ShapeDtypeStruct((batch * seq, GLA_V_WIDTH), BF16)] * 2 + [
        jax.ShapeDtypeStruct((batch, GLA_HEADS, GLA_DV, GLA_DK), F32)] * 2
    return pl.pallas_call(
        _gla_kernel,
        grid=(batch, GLA_HEADS // hps, nc),
        in_specs=in_specs,
        out_specs=out_specs,
        out_shape=out_shape,
        scratch_shapes=[pltpu.VMEM((hps, GLA_DV, GLA_DK), F32), pltpu.VMEM((hps, GLA_DV, GLA_DK), F32),
                        pltpu.VMEM((hps, c * GLA_SUB, GLA_DK), BF16),
                        pltpu.VMEM((hps, c * GLA_SUB, GLA_DK), BF16),
                        pltpu.VMEM((hps, c, GLA_DK), F32), pltpu.VMEM((hps, c, GLA_DK), F32)],
        compiler_params=_params("parallel", "parallel", "arbitrary"),
        name="gla_scan",
    )(p, p, p, lr, p, p, p, lr, wgf, bgf, wgb, bgb, masks, s0f, s0b)


def _merge_kernel(oa_ref, of_ref, ob_ref, rb_ref, ga_ref, gb_ref, gn_ref, wa_ref, wg_ref, z_ref):
    og = of_ref[...].astype(F32) + ob_ref[...].astype(F32)
    parts = []
    for hh in range(GLA_HEADS):
        a = og[:, hh * GLA_DV:(hh + 1) * GLA_DV]
        ms = jnp.mean(a * a, axis=-1, keepdims=True)
        parts.append(a * lax.rsqrt(ms + EPS) * gn_ref[...])
    n = jnp.concatenate(parts, axis=1)
    rb = rb_ref[...].astype(F32)
    n = (n * (rb * _sigmoid(rb))).astype(BF16)
    y_gla = jnp.dot(n, wg_ref[...], preferred_element_type=F32)
    y_att = jnp.dot(oa_ref[...], wa_ref[...], preferred_element_type=F32)
    z = _sigmoid(ga_ref[...].astype(F32)) * y_att + _sigmoid(gb_ref[...].astype(F32)) * y_gla
    z_ref[...] = z.astype(z_ref.dtype)


def _merge(o_attn, o_f, o_b, p, gla_norm, w_attn_o, w_gla_o, tm=256):
    m, d = o_attn.shape
    row = lambda i: (i, 0)
    const = lambda i: (0, 0)
    wspec = pl.BlockSpec((d, d), const, pipeline_mode=pl.Buffered(1))
    return pl.pallas_call(
        _merge_kernel,
        grid=(m // tm,),
        in_specs=[pl.BlockSpec((tm, d), row), pl.BlockSpec((tm, d), row), pl.BlockSpec((tm, d), row),
                  pl.BlockSpec((tm, d), lambda i: (i, P_RB // d)),
                  pl.BlockSpec((tm, d), lambda i: (i, P_GA // d)),
                  pl.BlockSpec((tm, d), lambda i: (i, P_GB // d)),
                  pl.BlockSpec((1, GLA_DV), const), wspec, wspec],
        out_specs=pl.BlockSpec((tm, d), row),
        out_shape=jax.ShapeDtypeStruct((m, d), BF16),
        compiler_params=_params("parallel"),
        name="merge_gates",
    )(o_attn, o_f, o_b, p, p, p, gla_norm.reshape(1, GLA_DV), w_attn_o, w_gla_o)


def _out_proj_kernel(z_ref, w_ref, x_ref, gate_ref, o_ref):
    y = jnp.dot(z_ref[...], w_ref[...], preferred_element_type=F32)
    o_ref[...] = x_ref[...] + gate_ref[0] * y


def _out_proj(z, w_out, x2, gate, rows_per_batch, tm=512):
    m, d = x2.shape
    tm = min(tm, rows_per_batch)
    tpb = rows_per_batch // tm
    return pl.pallas_call(
        _out_proj_kernel,
        grid=(m // tm,),
        in_specs=[pl.BlockSpec((tm, d), lambda i: (i, 0)),
                  pl.BlockSpec((d, d), lambda i: (0, 0), pipeline_mode=pl.Buffered(1)),
                  pl.BlockSpec((tm, d), lambda i: (i, 0)),
                  pl.BlockSpec((1, 1, d), lambda i: (i // tpb, 0, 0))],
        out_specs=pl.BlockSpec((tm, d), lambda i: (i, 0)),
        out_shape=jax.ShapeDtypeStruct((m, d), F32),
        compiler_params=_params("parallel"),
        name="out_proj_residual",
    )(z, w_out, x2, gate)


FFN_SUB = 256
FFN_SUBS = 2
FFN_HALO = 16
FFN_OUT_COLS = 512


def _ffn_kernel(x_ref, xp_ref, xn_ref, shift_ref, scale_ref, gate_ref, gain_ref, wu_ref, cw_ref, cb_ref,
                wwin_ref, wtail_ref, o_ref, h_ref, carry_ref, *, tpb):
    i = pl.program_id(0)
    kk = pl.program_id(1)
    tm = x_ref.shape[0]
    d = x_ref.shape[1]
    ext = tm + 2 * FFN_HALO
    w2 = 2 * FFN_SUB

    def modnorm(x):
        ms = jnp.mean(x * x, axis=-1, keepdims=True)
        y = x * lax.rsqrt(ms + EPS) * gain_ref[...]
        return y * (1.0 + scale_ref[0]) + shift_ref[0]

    def activation(u, s):
        cols = slice(s * w2, (s + 1) * w2)
        rows = slice(FFN_HALO, FFN_HALO + tm)
        conv = (cw_ref[0:1, cols] * pltpu.roll(u, 1, axis=0)[rows]
                + cw_ref[1:2, cols] * u[rows]
                + cw_ref[2:3, cols] * pltpu.roll(u, ext - 1, axis=0)[rows] + cb_ref[:, cols])
        a = conv[:, :FFN_SUB]
        return (a * _sigmoid(a) * conv[:, FFN_SUB:]).astype(BF16)

    def project(act, w_rows):
        for nb in range(d // FFN_OUT_COLS):
            oc = slice(nb * FFN_OUT_COLS, (nb + 1) * FFN_OUT_COLS)
            o_ref[:, oc] += jnp.dot(act, w_rows(oc), preferred_element_type=F32)

    @pl.when(kk == 0)
    def _():
        keep_prev = jnp.where((i % tpb) == 0, 0.0, 1.0)
        keep_next = jnp.where((i % tpb) == tpb - 1, 0.0, 1.0)
        h_ref[0:FFN_HALO, :] = (modnorm(xp_ref[...]) * keep_prev).astype(BF16)
        h_ref[FFN_HALO:FFN_HALO + tm, :] = modnorm(x_ref[...]).astype(BF16)
        h_ref[FFN_HALO + tm:ext, :] = (modnorm(xn_ref[...]) * keep_next).astype(BF16)
        o_ref[...] = jnp.zeros_like(o_ref)
        carry_ref[...] = jnp.zeros_like(carry_ref)

    us = [jnp.dot(h_ref[...], wu_ref[:, s * w2:(s + 1) * w2], preferred_element_type=F32)
          for s in range(FFN_SUBS)]
    act0 = activation(us[0], 0)
    project(jnp.concatenate([carry_ref[(kk + 1) % 2], act0], axis=1), lambda oc: wwin_ref[:, oc])
    carry_ref[kk % 2] = activation(us[1], 1)

    @pl.when(kk == pl.num_programs(1) - 1)
    def _():
        project(carry_ref[kk % 2], lambda oc: wtail_ref[0:FFN_SUB, oc])
        o_ref[...] = x_ref[...] + gate_ref[0] * o_ref[...]


def _conv_ffn(x2, shift, scale, gate, gain, w_up_r, conv_w_r, conv_b_r, w_down, rows_per_batch, tm=1024):
    m, d = x2.shape
    dff = w_down.shape[0]
    tm = min(tm, rows_per_batch)
    tpb = rows_per_batch // tm
    hb = tm // FFN_HALO
    nhalo = m // FFN_HALO
    tk = FFN_SUB * FFN_SUBS
    nk = dff // tk
    once = pl.Buffered(1)
    zpad = jnp.zeros((FFN_SUB, d), w_down.dtype)
    w_down_pad = jnp.concatenate([zpad, w_down, zpad], axis=0)
    return pl.pallas_call(
        functools.partial(_ffn_kernel, tpb=tpb),
        grid=(m // tm, nk),
        in_specs=[pl.BlockSpec((tm, d), lambda i, k: (i, 0), pipeline_mode=once),
                  pl.BlockSpec((FFN_HALO, d), lambda i, k: (jnp.maximum(i * hb - 1, 0), 0)),
                  pl.BlockSpec((FFN_HALO, d), lambda i, k: (jnp.minimum((i + 1) * hb, nhalo - 1), 0)),
                  pl.BlockSpec((1, 1, d), lambda i, k: (i // tpb, 0, 0)),
                  pl.BlockSpec((1, 1, d), lambda i, k: (i // tpb, 0, 0)),
                  pl.BlockSpec((1, 1, d), lambda i, k: (i // tpb, 0, 0)),
                  pl.BlockSpec((1, d), lambda i, k: (0, 0)),
                  pl.BlockSpec((d, 2 * tk), lambda i, k: (0, k)),
                  pl.BlockSpec((3, 2 * tk), lambda i, k: (0, k)),
                  pl.BlockSpec((1, 2 * tk), lambda i, k: (0, k)),
                  pl.BlockSpec((tk, d), lambda i, k: (k, 0)),
                  pl.BlockSpec((tk, d), lambda i, k: (nk, 0), pipeline_mode=once)],
        out_specs=pl.BlockSpec((tm, d), lambda i, k: (i, 0)),
        out_shape=jax.ShapeDtypeStruct((m, d), F32),
        scratch_shapes=[pltpu.VMEM((tm + 2 * FFN_HALO, d), BF16), pltpu.VMEM((2, tm, FFN_SUB), BF16)],
        compiler_params=_params("parallel", "arbitrary"),
        name="conv_ffn",
    )(x2, x2, x2, shift, scale, gate, gain.reshape(1, d), w_up_r, conv_w_r, conv_b_r, w_down_pad,
      w_down_pad)


def _regroup_ffn_cols(a):
    lead = a.shape[:-1]
    nblk = D_FF // FFN_SUB
    a = a.reshape(*lead, 2, nblk, FFN_SUB)
    return jnp.swapaxes(a, -3, -2).reshape(*lead, 2 * D_FF)


def _rope_tables(n):
    rows = n // GRID_W
    row = jnp.repeat(jnp.arange(rows), GRID_W)
    col = jnp.tile(jnp.arange(GRID_W), rows)
    n_freq = HEAD_DIM // 4
    inv = ROPE_THETA ** (-jnp.arange(n_freq, dtype=F32) / n_freq)
    ang = jnp.concatenate([row[:, None] * inv, col[:, None] * inv], axis=-1)
    cos, sin = jnp.cos(ang), jnp.sin(ang)
    return jnp.concatenate([cos, cos], axis=-1), jnp.concatenate([-sin, sin], axis=-1)


def _split_w_in(w_in):
    names = ("qa", "ka", "va", "qb", "kb", "vb", "rb", "lrf", "lrb", "ga", "gb")
    splits = (ATTN_WIDTH, KV_WIDTH, KV_WIDTH, GLA_K_WIDTH, GLA_K_WIDTH, GLA_V_WIDTH, GLA_V_WIDTH,
              GLA_LOWRANK, GLA_LOWRANK, D_MODEL, D_MODEL)
    start = dict(zip(names, np.cumsum((0,) + splits[:-1]).tolist()))
    width = dict(zip(names, splits))
    bw = STAGE_COLS
    wt = jnp.swapaxes(w_in, 0, 1)

    def starts(group):
        out = []
        for nm in group:
            assert start[nm] % 8 == 0 and width[nm] % bw == 0
            out += [start[nm] + t * bw for t in range(width[nm] // bw)]
        return out

    w_qk = _transpose_cast_rows(wt, starts(("qa", "ka")), bw)
    w_p = _transpose_cast_rows(wt, starts(("qb", "kb", "vb", "rb", "ga", "gb", "va")), bw)
    lr0, nlr = start["lrf"], 2 * GLA_LOWRANK
    pad = jnp.zeros((w_in.shape[0], LANES - nlr), w_in.dtype)
    w_lr = jnp.concatenate([jnp.swapaxes(wt[lr0:lr0 + nlr], 0, 1), pad], axis=-1).astype(BF16)
    return w_qk, w_p, w_lr


def kernel(x, c, ctx, c_ctx, w_mod, b_mod, g_mix, w_in, q_norm, k_norm, attn_sink, w_gate_f, b_gate_f,
           w_gate_b, b_gate_b, gla_norm, w_attn_o, w_gla_o, w_out, g_ffn, w_up, conv_w, conv_b, w_down):
    batch, seq, d = x.shape
    ctx_len = ctx.shape[1]
    assert w_mod.shape[0] == 1, "single-layer kernel"
    assert d == D_MODEL and seq % GLA_CHUNK == 0 and ctx_len % GLA_CHUNK == 0 and batch <= 7

    cc = jnp.zeros((8, d), F32).at[:batch].set(c).at[batch].set(c_ctx)
    mod = _modulation(cc, w_mod[0], b_mod[0]).reshape(8, 6, d)
    mod_x = [mod[:batch, jj][:, None, :] for jj in range(6)]
    mod_c = [mod[batch:batch + 1, jj][:, None, :] for jj in range(6)]

    w_qk, w_p, w_lr = _split_w_in(w_in[0])
    q_fold = HEAD_DIM ** -0.5 * LOG2E
    qk_norm_w = jnp.concatenate([jnp.tile(q_norm[0] * q_fold, N_Q_HEADS), jnp.tile(k_norm[0], N_KV_HEADS)])
    p_scale = jnp.ones((P_WIDTH,), F32).at[P_QB:P_QB + GLA_K_WIDTH].set(GLA_DK ** -0.5)
    cos2, sin2 = _rope_tables(seq)

    x2 = x.reshape(batch * seq, d)
    c2 = ctx.reshape(batch * ctx_len, d)
    qk, lr = _norm_matmul(x2, mod_x[0], mod_x[1], g_mix[0], w_qk, qk_norm_w, mode="qk",
                          rows_per_batch=seq, cos2=cos2, sin2=sin2, w_lr=w_lr, tm=512, tn=QK_WIDTH)
    p = _norm_matmul(x2, mod_x[0], mod_x[1], g_mix[0], w_p, p_scale, mode="plain", rows_per_batch=seq,
                     tn=1536)
    qk_c, lr_c = _norm_matmul(c2, mod_c[0], mod_c[1], g_mix[0], w_qk, qk_norm_w, mode="qk",
                              rows_per_batch=batch * ctx_len, w_lr=w_lr, tm=512, tn=QK_WIDTH)
    p_c = _norm_matmul(c2, mod_c[0], mod_c[1], g_mix[0], w_p, p_scale, mode="plain",
                       rows_per_batch=batch * ctx_len)

    o_attn = _attention(qk, p, qk_c, p_c, attn_sink[0], batch, seq, ctx_len)

    wgf = jnp.zeros((LANES, GLA_K_WIDTH), F32).at[:GLA_LOWRANK].set(w_gate_f[0])
    wgb = jnp.zeros((LANES, GLA_K_WIDTH), F32).at[GLA_LOWRANK:2 * GLA_LOWRANK].set(w_gate_b[0])
    bgf = b_gate_f[0].reshape(1, GLA_K_WIDTH)
    bgb = b_gate_b[0].reshape(1, GLA_K_WIDTH)
    zero_state = jnp.zeros((batch, GLA_HEADS, GLA_DV, GLA_DK), F32)
    _, _, sf, sb = _gla(p_c, lr_c, wgf, bgf, wgb, bgb, zero_state, zero_state, batch, ctx_len)
    o_f, o_b, _, _ = _gla(p, lr, wgf, bgf, wgb, bgb, sf, sb, batch, seq)

    z = _merge(o_attn, o_f, o_b, p, gla_norm[0], w_attn_o[0].astype(BF16), w_gla_o[0].astype(BF16))
    x1 = _out_proj(z, w_out[0].astype(BF16), x2, mod_x[2], seq)

    nblk = D_FF // FFN_SUB
    w_up_r = _permute_cast_cols(w_up[0], [(j % 2) * nblk + j // 2 for j in range(2 * nblk)], FFN_SUB)
    out = _conv_ffn(x1, mod_x[3], mod_x[4], mod_x[5], g_ffn[0], w_up_r,
                    _regroup_ffn_cols(conv_w[0]), _regroup_ffn_cols(conv_b[0]).reshape(1, 2 * D_FF),
                    w_down[0].astype(BF16), seq)
    return out.reshape(batch, seq, d)
```

```python
import functools

import numpy as np
import jax
import jax.numpy as jnp
from jax import lax
from jax.experimental import pallas as pl
from jax.experimental.pallas import tpu as pltpu

F32 = jnp.float32
BF16 = jnp.bfloat16

D_MODEL = 2048
GRID_W = 64
HEAD_DIM = 128
N_Q_HEADS = 16
N_KV_HEADS = 4
Q_PER_KV = N_Q_HEADS // N_KV_HEADS
WINDOW = 128
ROPE_THETA = 10000.0
GLA_HEADS = 4
GLA_DK = D_MODEL // 2 // GLA_HEADS
GLA_DV = D_MODEL // GLA_HEADS
GLA_LOWRANK = 16
GLA_GATE_NORM = 16.0
D_FF = 5632
EPS = 1e-6
ATTN_WIDTH = N_Q_HEADS * HEAD_DIM
KV_WIDTH = N_KV_HEADS * HEAD_DIM
GLA_K_WIDTH = GLA_HEADS * GLA_DK
GLA_V_WIDTH = GLA_HEADS * GLA_DV

LANES = 128
MXU_WIDTH = 256
VMEM_LIMIT_BYTES = 60 * 1024 * 1024

GLA_CHUNK = 128
GLA_SUB = 8
GLA_LEVELS = (64, 32, 16, 8)
LOG2E = 1.4426950408889634
GLA_HEADS_PER_STEP = 4

P_QB, P_KB, P_VB, P_RB, P_GA, P_GB, P_VA = 0, 1024, 2048, 4096, 6144, 8192, 10240
P_WIDTH = 10752
QK_WIDTH = ATTN_WIDTH + KV_WIDTH
QK_GROUP = 512
STAGE_COLS = 512


def _params(*sem):
    return pltpu.CompilerParams(dimension_semantics=sem, vmem_limit_bytes=VMEM_LIMIT_BYTES)


def _sigmoid(x):
    return 1.0 / (1.0 + jnp.exp(-x))


def _permute_cast_kernel(src_ref, a_ref, o_ref):
    o_ref[...] = a_ref[...].astype(o_ref.dtype)


def _permute_cast_cols(w, src_blocks, bw):
    kdim = w.shape[0]
    n = len(src_blocks)
    src = jnp.asarray(np.asarray(src_blocks, np.int32))
    return pl.pallas_call(
        _permute_cast_kernel,
        grid_spec=pltpu.PrefetchScalarGridSpec(
            num_scalar_prefetch=1,
            grid=(n,),
            in_specs=[pl.BlockSpec((kdim, bw), lambda j, s: (0, s[j]))],
            out_specs=pl.BlockSpec((kdim, bw), lambda j, s: (0, j)),
        ),
        out_shape=jax.ShapeDtypeStruct((kdim, n * bw), BF16),
        compiler_params=_params("arbitrary"),
        name="permute_cast",
    )(src, w)


def _transpose_cast_kernel(start_ref, a_ref, o_ref):
    o_ref[...] = a_ref[...].T.astype(o_ref.dtype)


def _transpose_cast_rows(wt, row_starts, bw):
    kdim = wt.shape[1]
    n = len(row_starts)
    sub = 8
    assert all(r % sub == 0 for r in row_starts)
    starts = jnp.asarray(np.asarray(row_starts, np.int32) // sub)
    return pl.pallas_call(
        _transpose_cast_kernel,
        grid_spec=pltpu.PrefetchScalarGridSpec(
            num_scalar_prefetch=1,
            grid=(n,),
            in_specs=[pl.BlockSpec((pl.Element(bw), pl.Element(kdim)), lambda j, s: (s[j] * sub, 0))],
            out_specs=pl.BlockSpec((kdim, bw), lambda j, s: (0, j)),
        ),
        out_shape=jax.ShapeDtypeStruct((kdim, n * bw), BF16),
        compiler_params=_params("arbitrary"),
        name="transpose_cast",
    )(starts, wt)


def _mod_kernel(a_ref, w_ref, b_ref, o_ref):
    a = a_ref[...]
    s = (a * _sigmoid(a)).astype(BF16)
    o_ref[...] = jnp.dot(s, w_ref[...].astype(BF16), preferred_element_type=F32) + b_ref[...]


def _modulation(cc, w_mod, b_mod):
    d, n = w_mod.shape
    tn = 1024
    return pl.pallas_call(
        _mod_kernel,
        grid=(n // tn,),
        in_specs=[pl.BlockSpec((8, d), lambda j: (0, 0)),
                  pl.BlockSpec((d, tn), lambda j: (0, j)),
                  pl.BlockSpec((1, tn), lambda j: (0, j))],
        out_specs=pl.BlockSpec((8, tn), lambda j: (0, j)),
        out_shape=jax.ShapeDtypeStruct((8, n), F32),
        compiler_params=_params("parallel"),
        name="modulation",
    )(cc, w_mod, b_mod.reshape(1, n))


def _norm_mm_kernel(*refs, mode, rope, has_lr, tn):
    it = iter(refs)
    x_ref, shift_ref, scale_ref, gain_ref, w_ref, cs_ref = [next(it) for _ in range(6)]
    cos_ref = sin_ref = wlr_ref = lr_ref = seg_ref = perm_ref = None
    if mode == "qk":
        seg_ref, perm_ref = next(it), next(it)
    if rope:
        cos_ref, sin_ref = next(it), next(it)
    if has_lr:
        wlr_ref = next(it)
    o_ref = next(it)
    if has_lr:
        lr_ref = next(it)
    h_ref = next(it)

    @pl.when(pl.program_id(1) == 0)
    def _():
        x = x_ref[...]
        ms = jnp.mean(x * x, axis=-1, keepdims=True)
        y = x * lax.rsqrt(ms + EPS) * gain_ref[...]
        h = (y * (1.0 + scale_ref[0]) + shift_ref[0]).astype(BF16)
        h_ref[...] = h
        if has_lr:
            lr_ref[...] = jnp.dot(h, wlr_ref[...], preferred_element_type=F32)

    if mode == "plain":
        acc = jnp.dot(h_ref[...], w_ref[...], preferred_element_type=F32)
        o_ref[...] = (acc * cs_ref[...]).astype(o_ref.dtype)
    else:
        for gb in range(tn // QK_GROUP):
            big = jnp.dot(h_ref[...], w_ref[:, gb * QK_GROUP:(gb + 1) * QK_GROUP],
                          preferred_element_type=F32)
            for cb in range(QK_GROUP // MXU_WIDTH):
                c0 = gb * QK_GROUP + cb * MXU_WIDTH
                acc = big[:, cb * MXU_WIDTH:(cb + 1) * MXU_WIDTH]
                ss = jnp.dot((acc * acc).astype(BF16), seg_ref[...], preferred_element_type=F32)
                an = acc * lax.rsqrt(ss * (1.0 / HEAD_DIM) + EPS) * cs_ref[:, c0:c0 + MXU_WIDTH]
                if rope:
                    rot = jnp.dot(an.astype(BF16), perm_ref[...], preferred_element_type=F32)
                for hh in range(MXU_WIDTH // HEAD_DIM):
                    hs = slice(hh * HEAD_DIM, (hh + 1) * HEAD_DIM)
                    a = an[:, hs]
                    if rope:
                        a = a * cos_ref[...] + rot[:, hs] * sin_ref[...]
                    o_ref[:, c0 + hh * HEAD_DIM:c0 + (hh + 1) * HEAD_DIM] = a.astype(o_ref.dtype)


def _norm_matmul(x2, shift, scale, gain, w, colvec, *, mode, rows_per_batch, cos2=None, sin2=None,
                 w_lr=None, tm=1024, tn=512):
    m, d = x2.shape
    n = w.shape[1]
    tm = min(tm, rows_per_batch)
    tpb = rows_per_batch // tm
    rope = cos2 is not None
    has_lr = w_lr is not None
    in_specs = [pl.BlockSpec((tm, d), lambda i, j: (i, 0)),
                pl.BlockSpec((1, 1, d), lambda i, j: (i // tpb, 0, 0)),
                pl.BlockSpec((1, 1, d), lambda i, j: (i // tpb, 0, 0)),
                pl.BlockSpec((1, d), lambda i, j: (0, 0)),
                pl.BlockSpec((d, tn), lambda i, j: (0, j)),
                pl.BlockSpec((1, tn), lambda i, j: (0, j))]
    args = [x2, shift, scale, gain.reshape(1, d), w, colvec.reshape(1, n)]
    if mode == "qk":
        lane = np.arange(MXU_WIDTH)
        same_head = lane[:, None] // HEAD_DIM == lane[None, :] // HEAD_DIM
        rolled = (lane[:, None] % HEAD_DIM) == ((lane[None, :] - HEAD_DIM // 2) % HEAD_DIM)
        in_specs += [pl.BlockSpec((MXU_WIDTH, MXU_WIDTH), lambda i, j: (0, 0))] * 2
        args += [jnp.asarray(same_head, BF16), jnp.asarray(same_head & rolled, BF16)]
    if rope:
        in_specs += [pl.BlockSpec((tm, HEAD_DIM), lambda i, j: (i % tpb, 0))] * 2
        args += [cos2, sin2]
    out_shape = [jax.ShapeDtypeStruct((m, n), BF16)]
    out_specs = [pl.BlockSpec((tm, tn), lambda i, j: (i, j))]
    if has_lr:
        in_specs.append(pl.BlockSpec((d, LANES), lambda i, j: (0, 0)))
        args.append(w_lr)
        out_shape.append(jax.ShapeDtypeStruct((m, LANES), F32))
        out_specs.append(pl.BlockSpec((tm, LANES), lambda i, j: (i, 0)))
    res = pl.pallas_call(
        functools.partial(_norm_mm_kernel, mode=mode, rope=rope, has_lr=has_lr, tn=tn),
        grid=(m // tm, n // tn),
        in_specs=in_specs,
        out_specs=out_specs,
        out_shape=out_shape,
        scratch_shapes=[pltpu.VMEM((tm, d), BF16)],
        compiler_params=_params("parallel", "arbitrary"),
        name="norm_matmul_" + mode,
    )(*args)
    return res if has_lr else res[0]


ATTN_Q_BLOCKS = 2


def _attn_kernel(sink_ref, q_ref, *refs, nblk):
    nq = ATTN_Q_BLOCKS
    k_refs, kx_ref = refs[:nq + 2], refs[nq + 2]
    v_refs, vx_ref = refs[nq + 3:2 * nq + 5], refs[2 * nq + 5]
    o_ref = refs[2 * nq + 6]
    i = pl.program_id(1)
    blk = HEAD_DIM
    dn = (((1,), (1,)), ((), ()))
    row = lax.broadcasted_iota(jnp.int32, (blk, blk), 0)
    col = lax.broadcasted_iota(jnp.int32, (blk, blk), 1)
    ninf = jnp.float32(-jnp.inf)
    tri_p = jnp.concatenate([jnp.where(col >= row, 0.0, ninf)] * Q_PER_KV, axis=0)
    tri_n = jnp.concatenate([jnp.where(col <= row, 0.0, ninf)] * Q_PER_KV, axis=0)
    n_ctx = kx_ref.shape[0] // blk

    for a in range(nq):
        gblk = i * nq + a
        bias_p = tri_p + jnp.where(gblk > 0, 0.0, ninf)
        bias_n = tri_n + jnp.where(gblk < nblk - 1, 0.0, ninf)
        rows = slice(a * blk, (a + 1) * blk)
        for h in range(N_KV_HEADS):
            hs = slice(h * HEAD_DIM, (h + 1) * HEAD_DIM)
            qs = jnp.concatenate(
                [q_ref[rows, (h * Q_PER_KV + g) * HEAD_DIM:(h * Q_PER_KV + g + 1) * HEAD_DIM]
                 for g in range(Q_PER_KV)], axis=0)

            def scores(k_ref):
                return lax.dot_general(qs, k_ref[:, hs], dn, preferred_element_type=F32)

            s_x = scores(kx_ref)
            pieces = [scores(k_refs[a]) + bias_p, scores(k_refs[a + 1]), scores(k_refs[a + 2]) + bias_n]
            pieces += [s_x[:, j * blk:(j + 1) * blk] for j in range(n_ctx)]
            sink = jnp.concatenate(
                [jnp.full((blk, 1), sink_ref[h * Q_PER_KV + g] * LOG2E, F32) for g in range(Q_PER_KV)],
                axis=0)
            mx = pieces[0]
            for s in pieces[1:]:
                mx = jnp.maximum(mx, s)
            m = jnp.maximum(jnp.max(mx, axis=-1, keepdims=True), sink)
            probs = [jnp.exp2(s - m) for s in pieces]
            psum = probs[0]
            for pr in probs[1:]:
                psum = psum + pr
            denom = jnp.exp2(sink - m) + jnp.sum(psum, axis=-1, keepdims=True)
            p_x = jnp.concatenate(probs[3:], axis=1) if n_ctx > 1 else probs[3]
            o = (jnp.dot(probs[0].astype(BF16), v_refs[a][:, hs], preferred_element_type=F32)
                 + jnp.dot(probs[1].astype(BF16), v_refs[a + 1][:, hs], preferred_element_type=F32)
                 + jnp.dot(probs[2].astype(BF16), v_refs[a + 2][:, hs], preferred_element_type=F32)
                 + jnp.dot(p_x.astype(BF16), vx_ref[:, hs], preferred_element_type=F32))
            o = o / denom
            for g in range(Q_PER_KV):
                c0 = (h * Q_PER_KV + g) * HEAD_DIM
                o_ref[rows, c0:c0 + HEAD_DIM] = o[g * blk:(g + 1) * blk].astype(o_ref.dtype)


def _attention(qk, p, qk_c, p_c, sink, batch, seq, ctx_len):
    blk = HEAD_DIM
    nblk = seq // blk
    nq = ATTN_Q_BLOCKS
    nstep = nblk // nq
    kcol = ATTN_WIDTH // KV_WIDTH
    vcol = P_VA // KV_WIDTH

    def kv_specs(col0):
        def spec(shift):
            return pl.BlockSpec(
                (blk, KV_WIDTH), lambda b, i: (b * nblk + jnp.clip(i * nq + shift, 0, nblk - 1), col0))
        return [spec(shift) for shift in range(-1, nq + 1)]

    in_specs = ([pl.BlockSpec(memory_space=pltpu.SMEM),
                 pl.BlockSpec((nq * blk, ATTN_WIDTH), lambda b, i: (b * nstep + i, 0))]
                + kv_specs(kcol) + [pl.BlockSpec((ctx_len, KV_WIDTH), lambda b, i: (b, kcol))]
                + kv_specs(vcol) + [pl.BlockSpec((ctx_len, KV_WIDTH), lambda b, i: (b, vcol))])
    return pl.pallas_call(
        functools.partial(_attn_kernel, nblk=nblk),
        grid=(batch, nstep),
        in_specs=in_specs,
        out_specs=pl.BlockSpec((nq * blk, ATTN_WIDTH), lambda b, i: (b * nstep + i, 0)),
        out_shape=jax.ShapeDtypeStruct((batch * seq, ATTN_WIDTH), BF16),
        compiler_params=_params("parallel", "arbitrary"),
        name="window_attention",
    )(sink, qk, *([qk] * (nq + 2)), qk_c, *([p] * (nq + 2)), p_c)


def _gla_masks():
    c = GLA_CHUNK
    t = np.arange(c)[:, None]
    s = np.arange(c)[None, :]
    fwd = []
    for w in GLA_LEVELS:
        fwd.append((t // (2 * w) == s // (2 * w)) & (t % (2 * w) >= w) & (s % (2 * w) < w))
    fwd.append((t // GLA_SUB == s // GLA_SUB) & (s <= t))
    fwd.append(s <= t)
    fwd = np.stack(fwd).astype(np.float32)
    return np.stack([fwd, np.transpose(fwd, (0, 2, 1))])


def _gla_chunk(q_ref, k_ref, v_ref, lr_parts, wg_ref, bg_ref, mask_ref, st_ref, w_ref, b_ref, o_ref, reverse, hp):
    c = GLA_CHUNK
    nl = len(GLA_LEVELS)
    ksl = slice(hp * GLA_DK, (hp + 1) * GLA_DK)
    vsl = slice(hp * GLA_DV, (hp + 1) * GLA_DV)
    q = q_ref[:, ksl].astype(F32)
    kb = k_ref[:, ksl]
    k = kb.astype(F32)
    v = v_ref[:, vsl]
    lr_hi, lr_lo = lr_parts
    wg_hi, wg_lo = wg_ref[0, :, ksl], wg_ref[1, :, ksl]
    xg = (jnp.dot(lr_hi, wg_hi, preferred_element_type=F32) + jnp.dot(lr_hi, wg_lo, preferred_element_type=F32)
          + jnp.dot(lr_lo, wg_hi, preferred_element_type=F32)) + bg_ref[:, ksl]
    yield
    g = (jnp.minimum(xg, 0.0) - jnp.log1p(jnp.exp(-jnp.abs(xg)))) * (LOG2E / GLA_GATE_NORM)

    tri = mask_ref[nl + 1].astype(BF16)
    g_hi = g.astype(BF16)
    r1 = g - g_hi.astype(F32)
    g_mid = r1.astype(BF16)
    g_lo = (r1 - g_mid.astype(F32)).astype(BF16)
    yield
    b = (jnp.dot(tri, g_hi, preferred_element_type=F32) + jnp.dot(tri, g_mid, preferred_element_type=F32)
         + jnp.dot(tri, g_lo, preferred_element_type=F32))
    b_ref[...] = b
    yield

    dn_nt = (((1,), (1,)), ((), ()))
    a_mat = None
    for lvl, w in enumerate(GLA_LEVELS):
        zero = jnp.zeros((w, GLA_DK), F32)
        qparts, kparts = [], []
        for mblk in range(c // (2 * w)):
            lo, mid, hi = mblk * 2 * w, mblk * 2 * w + w, mblk * 2 * w + 2 * w
            if reverse:
                r = b[mid:mid + 1, :]
                qparts += [q[lo:mid] * jnp.exp2(b[lo:mid] - r), zero]
                kparts += [zero, k[mid:hi] * jnp.exp2(r - b[mid:hi])]
            else:
                r = b[mid - 1:mid, :]
                qparts += [zero, q[mid:hi] * jnp.exp2(b[mid:hi] - r)]
                kparts += [k[lo:mid] * jnp.exp2(r - b[lo:mid]), zero]
        qn = jnp.concatenate(qparts, axis=0).astype(BF16)
        kn = jnp.concatenate(kparts, axis=0).astype(BF16)
        term = lax.dot_general(qn, kn, dn_nt, preferred_element_type=F32)
        if 2 * w < c:
            term = term * mask_ref[lvl]
        a_mat = term if a_mat is None else a_mat + term
        yield

    for i in range(c // GLA_SUB):
        r0 = i * GLA_SUB
        qi = q[r0:r0 + GLA_SUB]
        bi = b[r0:r0 + GLA_SUB]
        for s in range(0, GLA_SUB, 2):
            e0 = jnp.exp2(bi - b_ref[pl.ds(r0 + s, 1), :])
            e1 = jnp.exp2(bi - b_ref[pl.ds(r0 + s + 1, 1), :])
            w_ref[pl.ds((r0 + s) * GLA_SUB, 2 * GLA_SUB), :] = jnp.concatenate(
                [qi * e0, qi * e1], axis=0).astype(BF16)
        yield
    red = lax.dot_general(w_ref[...], kb, dn_nt, preferred_element_type=F32)
    yield
    lane = lax.broadcasted_iota(jnp.int32, (GLA_SUB, c), 1)
    rows = []
    for i in range(c // GLA_SUB):
        blk = jnp.zeros((GLA_SUB, c), F32)
        for s in range(GLA_SUB):
            col = i * GLA_SUB + s
            blk = jnp.where(lane == col, red[col * GLA_SUB:(col + 1) * GLA_SUB], blk)
        rows.append(blk)
        if i % 4 == 3:
            yield
    a_mat = a_mat + jnp.where(mask_ref[nl] > 0.5, jnp.concatenate(rows, axis=0), 0.0)

    st = st_ref[...]
    qe = (q * jnp.exp2(b)).astype(BF16)
    o = lax.dot_general(qe, st.astype(BF16), dn_nt, preferred_element_type=F32)
    yield
    o = o + jnp.dot(a_mat.astype(BF16), v, preferred_element_type=F32)
    o_ref[:, vsl] = o.astype(o_ref.dtype)
    yield

    r_end = 0 if reverse else c - 1
    b_end = b[r_end:r_end + 1, :]
    ke = (k * jnp.exp2(b_end - b)).astype(BF16)
    upd = lax.dot_general(v, ke, (((0,), (0,)), ((), ())), preferred_element_type=F32)
    yield
    st_ref[...] = st * jnp.exp2(b_end) + upd


def _gla_kernel(qf_ref, kf_ref, vf_ref, lrf_ref, qb_ref, kb_ref, vb_ref, lrb_ref,
                wgf_ref, bgf_ref, wgb_ref, bgb_ref, mask_ref, s0f_ref, s0b_ref,
                of_ref, ob_ref, sf_ref, sb_ref, stf_ref, stb_ref, wf_ref, wb_ref, bf_ref, bb_ref):
    cidx = pl.program_id(2)

    @pl.when(cidx == 0)
    def _():
        stf_ref[...] = s0f_ref[0]
        stb_ref[...] = s0b_ref[0]

    def hi_lo(ref):
        x = ref[...]
        hi = x.astype(BF16)
        return hi, (x - hi.astype(F32)).astype(BF16)

    lrf, lrb = hi_lo(lrf_ref), hi_lo(lrb_ref)
    chains = []
    for hp in range(GLA_HEADS_PER_STEP):
        chains.append(_gla_chunk(qf_ref, kf_ref, vf_ref, lrf, wgf_ref, bgf_ref, mask_ref.at[0],
                                 stf_ref.at[hp], wf_ref.at[hp], bf_ref.at[hp], of_ref, False, hp))
        chains.append(_gla_chunk(qb_ref, kb_ref, vb_ref, lrb, wgb_ref, bgb_ref, mask_ref.at[1],
                                 stb_ref.at[hp], wb_ref.at[hp], bb_ref.at[hp], ob_ref, True, hp))
    while chains:
        chains = [ch for ch in chains if next(ch, True) is None]

    @pl.when(cidx == pl.num_programs(2) - 1)
    def _():
        sf_ref[0] = stf_ref[...]
        sb_ref[0] = stb_ref[...]


def _gla(p, lr, wgf, bgf, wgb, bgb, s0f, s0b, batch, seq):
    c = GLA_CHUNK
    nc = seq // c
    hps = GLA_HEADS_PER_STEP
    kw, vw = hps * GLA_DK, hps * GLA_DV
    kcol = P_KB // kw
    vcol = P_VB // vw

    def fwd(b, h, i):
        return b * nc + i

    def bwd(b, h, i):
        return b * nc + (nc - 1 - i)

    def data_specs(rowf):
        return [pl.BlockSpec((c, kw), lambda b, h, i: (rowf(b, h, i), h)),
                pl.BlockSpec((c, kw), lambda b, h, i: (rowf(b, h, i), kcol + h)),
                pl.BlockSpec((c, vw), lambda b, h, i: (rowf(b, h, i), vcol + h)),
                pl.BlockSpec((c, LANES), lambda b, h, i: (rowf(b, h, i), 0))]

    gate_specs = [pl.BlockSpec((2, LANES, kw), lambda b, h, i: (0, 0, h)),
                  pl.BlockSpec((1, kw), lambda b, h, i: (0, h))]
    state_spec = pl.BlockSpec((1, hps, GLA_DV, GLA_DK), lambda b, h, i: (b, h, 0, 0))
    masks = jnp.asarray(_gla_masks())
    in_specs = (data_specs(fwd) + data_specs(bwd) + gate_specs + gate_specs
                + [pl.BlockSpec(masks.shape, lambda b, h, i: (0, 0, 0, 0)), state_spec, state_spec])
    out_specs = [pl.BlockSpec((c, vw), lambda b, h, i: (fwd(b, h, i), h)),
                 pl.BlockSpec((c, vw), lambda b, h, i: (bwd(b, h, i), h)),
                 state_spec, state_spec]
    out_shape = [jax.ShapeDtypeStruct((batch * seq, GLA_V_WIDTH), BF16)] * 2 + [
        jax.ShapeDtypeStruct((batch, GLA_HEADS, GLA_DV, GLA_DK), F32)] * 2
    return pl.pallas_call(
        _gla_kernel,
        grid=(batch, GLA_HEADS // hps, nc),
        in_specs=in_specs,
        out_specs=out_specs,
        out_shape=out_shape,
        scratch_shapes=[pltpu.VMEM((hps, GLA_DV, GLA_DK), F32), pltpu.VMEM((hps, GLA_DV, GLA_DK), F32),
                        pltpu.VMEM((hps, c * GLA_SUB, GLA_DK), BF16),
                        pltpu.VMEM((hps, c * GLA_SUB, GLA_DK), BF16),
                        pltpu.VMEM((hps, c, GLA_DK), F32), pltpu.VMEM((hps, c, GLA_DK), F32)],
        compiler_params=_params("parallel", "parallel", "arbitrary"),
        name="gla_scan",
    )(p, p, p, lr, p, p, p, lr, wgf, bgf, wgb, bgb, masks, s0f, s0b)


def _merge_kernel(oa_ref, of_ref, ob_ref, rb_ref, ga_ref, gb_ref, gn_ref, wa_ref, wg_ref, z_ref):
    og = of_ref[...].astype(F32) + ob_ref[...].astype(F32)
    parts = []
    for hh in range(GLA_HEADS):
        a = og[:, hh * GLA_DV:(hh + 1) * GLA_DV]
        ms = jnp.mean(a * a, axis=-1, keepdims=True)
        parts.append(a * lax.rsqrt(ms + EPS) * gn_ref[...])
    n = jnp.concatenate(parts, axis=1)
    rb = rb_ref[...].astype(F32)
    n = (n * (rb * _sigmoid(rb))).astype(BF16)
    y_gla = jnp.dot(n, wg_ref[...], preferred_element_type=F32)
    y_att = jnp.dot(oa_ref[...], wa_ref[...], preferred_element_type=F32)
    z = _sigmoid(ga_ref[...].astype(F32)) * y_att + _sigmoid(gb_ref[...].astype(F32)) * y_gla
    z_ref[...] = z.astype(z_ref.dtype)


def _merge(o_attn, o_f, o_b, p, gla_norm, w_attn_o, w_gla_o, tm=256):
    m, d = o_attn.shape
    row = lambda i: (i, 0)
    const = lambda i: (0, 0)
    wspec = pl.BlockSpec((d, d), const, pipeline_mode=pl.Buffered(1))
    return pl.pallas_call(
        _merge_kernel,
        grid=(m // tm,),
        in_specs=[pl.BlockSpec((tm, d), row), pl.BlockSpec((tm, d), row), pl.BlockSpec((tm, d), row),
                  pl.BlockSpec((tm, d), lambda i: (i, P_RB // d)),
                  pl.BlockSpec((tm, d), lambda i: (i, P_GA // d)),
                  pl.BlockSpec((tm, d), lambda i: (i, P_GB // d)),
                  pl.BlockSpec((1, GLA_DV), const), wspec, wspec],
        out_specs=pl.BlockSpec((tm, d), row),
        out_shape=jax.ShapeDtypeStruct((m, d), BF16),
        compiler_params=_params("parallel"),
        name="merge_gates",
    )(o_attn, o_f, o_b, p, p, p, gla_norm.reshape(1, GLA_DV), w_attn_o, w_gla_o)


def _out_proj_kernel(z_ref, w_ref, x_ref, gate_ref, o_ref):
    y = jnp.dot(z_ref[...], w_ref[...], preferred_element_type=F32)
    o_ref[...] = x_ref[...] + gate_ref[0] * y


def _out_proj(z, w_out, x2, gate, rows_per_batch, tm=512):
    m, d = x2.shape
    tm = min(tm, rows_per_batch)
    tpb = rows_per_batch // tm
    return pl.pallas_call(
        _out_proj_kernel,
        grid=(m // tm,),
        in_specs=[pl.BlockSpec((tm, d), lambda i: (i, 0)),
                  pl.BlockSpec((d, d), lambda i: (0, 0), pipeline_mode=pl.Buffered(1)),
                  pl.BlockSpec((tm, d), lambda i: (i, 0)),
                  pl.BlockSpec((1, 1, d), lambda i: (i // tpb, 0, 0))],
        out_specs=pl.BlockSpec((tm, d), lambda i: (i, 0)),
        out_shape=jax.ShapeDtypeStruct((m, d), F32),
        compiler_params=_params("parallel"),
        name="out_proj_residual",
    )(z, w_out, x2, gate)


FFN_SUB = 256
FFN_SUBS = 2
FFN_HALO = 16
FFN_OUT_COLS = 512


def _ffn_kernel(x_ref, xp_ref, xn_ref, shift_ref, scale_ref, gate_ref, gain_ref, wu_ref, cw_ref, cb_ref,
                wd_ref, o_ref, h_ref, *, tpb):
    i = pl.program_id(0)
    kk = pl.program_id(1)
    tm = x_ref.shape[0]
    d = x_ref.shape[1]
    ext = tm + 2 * FFN_HALO

    def modnorm(x):
        ms = jnp.mean(x * x, axis=-1, keepdims=True)
        y = x * lax.rsqrt(ms + EPS) * gain_ref[...]
        return y * (1.0 + scale_ref[0]) + shift_ref[0]

    @pl.when(kk == 0)
    def _():
        keep_prev = jnp.where((i % tpb) == 0, 0.0, 1.0)
        keep_next = jnp.where((i % tpb) == tpb - 1, 0.0, 1.0)
        h_ref[0:FFN_HALO, :] = (modnorm(xp_ref[...]) * keep_prev).astype(BF16)
        h_ref[FFN_HALO:FFN_HALO + tm, :] = modnorm(x_ref[...]).astype(BF16)
        h_ref[FFN_HALO + tm:ext, :] = (modnorm(xn_ref[...]) * keep_next).astype(BF16)
        o_ref[...] = jnp.zeros_like(o_ref)

    w2 = 2 * FFN_SUB
    us = [jnp.dot(h_ref[...], wu_ref[:, s * w2:(s + 1) * w2], preferred_element_type=F32)
          for s in range(FFN_SUBS)]
    for s in range(FFN_SUBS):
        u = us[s]
        cols = slice(s * w2, (s + 1) * w2)
        rows = slice(FFN_HALO, FFN_HALO + tm)
        conv = (cw_ref[0:1, cols] * pltpu.roll(u, 1, axis=0)[rows]
                + cw_ref[1:2, cols] * u[rows]
                + cw_ref[2:3, cols] * pltpu.roll(u, ext - 1, axis=0)[rows] + cb_ref[:, cols])
        a = conv[:, :FFN_SUB]
        act = (a * _sigmoid(a) * conv[:, FFN_SUB:]).astype(BF16)
        for nb in range(d // FFN_OUT_COLS):
            oc = slice(nb * FFN_OUT_COLS, (nb + 1) * FFN_OUT_COLS)
            o_ref[:, oc] += jnp.dot(act, wd_ref[s * FFN_SUB:(s + 1) * FFN_SUB, oc],
                                    preferred_element_type=F32)

    @pl.when(kk == pl.num_programs(1) - 1)
    def _():
        o_ref[...] = x_ref[...] + gate_ref[0] * o_ref[...]


def _conv_ffn(x2, shift, scale, gate, gain, w_up_r, conv_w_r, conv_b_r, w_down, rows_per_batch, tm=1024):
    m, d = x2.shape
    dff = w_down.shape[0]
    tm = min(tm, rows_per_batch)
    tpb = rows_per_batch // tm
    hb = tm // FFN_HALO
    nhalo = m // FFN_HALO
    tk = FFN_SUB * FFN_SUBS
    nk = dff // tk
    once = pl.Buffered(1)
    return pl.pallas_call(
        functools.partial(_ffn_kernel, tpb=tpb),
        grid=(m // tm, nk),
        in_specs=[pl.BlockSpec((tm, d), lambda i, k: (i, 0), pipeline_mode=once),
                  pl.BlockSpec((FFN_HALO, d), lambda i, k: (jnp.maximum(i * hb - 1, 0), 0)),
                  pl.BlockSpec((FFN_HALO, d), lambda i, k: (jnp.minimum((i + 1) * hb, nhalo - 1), 0)),
                  pl.BlockSpec((1, 1, d), lambda i, k: (i // tpb, 0, 0)),
                  pl.BlockSpec((1, 1, d), lambda i, k: (i // tpb, 0, 0)),
                  pl.BlockSpec((1, 1, d), lambda i, k: (i // tpb, 0, 0)),
                  pl.BlockSpec((1, d), lambda i, k: (0, 0)),
                  pl.BlockSpec((d, 2 * tk), lambda i, k: (0, k)),
                  pl.BlockSpec((3, 2 * tk), lambda i, k: (0, k)),
                  pl.BlockSpec((1, 2 * tk), lambda i, k: (0, k)),
                  pl.BlockSpec((tk, d), lambda i, k: (k, 0))],
        out_specs=pl.BlockSpec((tm, d), lambda i, k: (i, 0)),
        out_shape=jax.ShapeDtypeStruct((m, d), F32),
        scratch_shapes=[pltpu.VMEM((tm + 2 * FFN_HALO, d), BF16)],
        compiler_params=_params("parallel", "arbitrary"),
        name="conv_ffn",
    )(x2, x2, x2, shift, scale, gate, gain.reshape(1, d), w_up_r, conv_w_r, conv_b_r, w_down)


def _regroup_ffn_cols(a):
    lead = a.shape[:-1]
    nblk = D_FF // FFN_SUB
    a = a.reshape(*lead, 2, nblk, FFN_SUB)
    return jnp.swapaxes(a, -3, -2).reshape(*lead, 2 * D_FF)


def _rope_tables(n):
    rows = n // GRID_W
    row = jnp.repeat(jnp.arange(rows), GRID_W)
    col = jnp.tile(jnp.arange(GRID_W), rows)
    n_freq = HEAD_DIM // 4
    inv = ROPE_THETA ** (-jnp.arange(n_freq, dtype=F32) / n_freq)
    ang = jnp.concatenate([row[:, None] * inv, col[:, None] * inv], axis=-1)
    cos, sin = jnp.cos(ang), jnp.sin(ang)
    return jnp.concatenate([cos, cos], axis=-1), jnp.concatenate([-sin, sin], axis=-1)


def _split_w_in(w_in):
    names = ("qa", "ka", "va", "qb", "kb", "vb", "rb", "lrf", "lrb", "ga", "gb")
    splits = (ATTN_WIDTH, KV_WIDTH, KV_WIDTH, GLA_K_WIDTH, GLA_K_WIDTH, GLA_V_WIDTH, GLA_V_WIDTH,
              GLA_LOWRANK, GLA_LOWRANK, D_MODEL, D_MODEL)
    start = dict(zip(names, np.cumsum((0,) + splits[:-1]).tolist()))
    width = dict(zip(names, splits))
    bw = STAGE_COLS
    wt = jnp.swapaxes(w_in, 0, 1)

    def starts(group):
        out = []
        for nm in group:
            assert start[nm] % 8 == 0 and width[nm] % bw == 0
            out += [start[nm] + t * bw for t in range(width[nm] // bw)]
        return out

    w_qk = _transpose_cast_rows(wt, starts(("qa", "ka")), bw)
    w_p = _transpose_cast_rows(wt, starts(("qb", "kb", "vb", "rb", "ga", "gb", "va")), bw)
    lr0, nlr = start["lrf"], 2 * GLA_LOWRANK
    pad = jnp.zeros((w_in.shape[0], LANES - nlr), w_in.dtype)
    w_lr = jnp.concatenate([jnp.swapaxes(wt[lr0:lr0 + nlr], 0, 1), pad], axis=-1).astype(BF16)
    return w_qk, w_p, w_lr


def kernel(x, c, ctx, c_ctx, w_mod, b_mod, g_mix, w_in, q_norm, k_norm, attn_sink, w_gate_f, b_gate_f,
           w_gate_b, b_gate_b, gla_norm, w_attn_o, w_gla_o, w_out, g_ffn, w_up, conv_w, conv_b, w_down):
    batch, seq, d = x.shape
    ctx_len = ctx.shape[1]
    assert w_mod.shape[0] == 1, "single-layer kernel"
    assert d == D_MODEL and seq % GLA_CHUNK == 0 and ctx_len % GLA_CHUNK == 0 and batch <= 7

    cc = jnp.zeros((8, d), F32).at[:batch].set(c).at[batch].set(c_ctx)
    mod = _modulation(cc, w_mod[0], b_mod[0]).reshape(8, 6, d)
    mod_x = [mod[:batch, jj][:, None, :] for jj in range(6)]
    mod_c = [mod[batch:batch + 1, jj][:, None, :] for jj in range(6)]

    w_qk, w_p, w_lr = _split_w_in(w_in[0])
    q_fold = HEAD_DIM ** -0.5 * LOG2E
    qk_norm_w = jnp.concatenate([jnp.tile(q_norm[0] * q_fold, N_Q_HEADS), jnp.tile(k_norm[0], N_KV_HEADS)])
    p_scale = jnp.ones((P_WIDTH,), F32).at[P_QB:P_QB + GLA_K_WIDTH].set(GLA_DK ** -0.5)
    cos2, sin2 = _rope_tables(seq)

    x2 = x.reshape(batch * seq, d)
    c2 = ctx.reshape(batch * ctx_len, d)
    qk, lr = _norm_matmul(x2, mod_x[0], mod_x[1], g_mix[0], w_qk, qk_norm_w, mode="qk",
                          rows_per_batch=seq, cos2=cos2, sin2=sin2, w_lr=w_lr, tm=512, tn=QK_WIDTH)
    p = _norm_matmul(x2, mod_x[0], mod_x[1], g_mix[0], w_p, p_scale, mode="plain", rows_per_batch=seq,
                     tn=1536)
    qk_c, lr_c = _norm_matmul(c2, mod_c[0], mod_c[1], g_mix[0], w_qk, qk_norm_w, mode="qk",
                              rows_per_batch=batch * ctx_len, w_lr=w_lr, tm=512, tn=QK_WIDTH)
    p_c = _norm_matmul(c2, mod_c[0], mod_c[1], g_mix[0], w_p, p_scale, mode="plain",
                       rows_per_batch=batch * ctx_len)

    o_attn = _attention(qk, p, qk_c, p_c, attn_sink[0], batch, seq, ctx_len)

    def hi_lo(w):
        hi = w.astype(BF16)
        return jnp.stack([hi, (w - hi.astype(F32)).astype(BF16)])

    wgf = hi_lo(jnp.zeros((LANES, GLA_K_WIDTH), F32).at[:GLA_LOWRANK].set(w_gate_f[0]))
    wgb = hi_lo(jnp.zeros((LANES, GLA_K_WIDTH), F32).at[GLA_LOWRANK:2 * GLA_LOWRANK].set(w_gate_b[0]))
    bgf = b_gate_f[0].reshape(1, GLA_K_WIDTH)
    bgb = b_gate_b[0].reshape(1, GLA_K_WIDTH)
    zero_state = jnp.zeros((batch, GLA_HEADS, GLA_DV, GLA_DK), F32)
    _, _, sf, sb = _gla(p_c, lr_c, wgf, bgf, wgb, bgb, zero_state, zero_state, batch, ctx_len)
    o_f, o_b, _, _ = _gla(p, lr, wgf, bgf, wgb, bgb, sf, sb, batch, seq)

    z = _merge(o_attn, o_f, o_b, p, gla_norm[0], w_attn_o[0].astype(BF16), w_gla_o[0].astype(BF16))
    x1 = _out_proj(z, w_out[0].astype(BF16), x2, mod_x[2], seq)

    nblk = D_FF // FFN_SUB
    w_up_r = _permute_cast_cols(w_up[0], [(j % 2) * nblk + j // 2 for j in range(2 * nblk)], FFN_SUB)
    out = _conv_ffn(x1, mod_x[3], mod_x[4], mod_x[5], g_ffn[0], w_up_r,
                    _regroup_ffn_cols(conv_w[0]), _regroup_ffn_cols(conv_b[0]).reshape(1, 2 * D_FF),
                    w_down[0].astype(BF16), seq)
    return out.reshape(batch, seq, d)
```

```python
import functools

import numpy as np
import jax
import jax.numpy as jnp
from jax import lax
from jax.experimental import pallas as pl
from jax.experimental.pallas import tpu as pltpu

F32 = jnp.float32
BF16 = jnp.bfloat16

D_MODEL = 2048
GRID_W = 64
HEAD_DIM = 128
N_Q_HEADS = 16
N_KV_HEADS = 4
Q_PER_KV = N_Q_HEADS // N_KV_HEADS
WINDOW = 128
ROPE_THETA = 10000.0
GLA_HEADS = 4
GLA_DK = D_MODEL // 2 // GLA_HEADS
GLA_DV = D_MODEL // GLA_HEADS
GLA_LOWRANK = 16
GLA_GATE_NORM = 16.0
D_FF = 5632
EPS = 1e-6
ATTN_WIDTH = N_Q_HEADS * HEAD_DIM
KV_WIDTH = N_KV_HEADS * HEAD_DIM
GLA_K_WIDTH = GLA_HEADS * GLA_DK
GLA_V_WIDTH = GLA_HEADS * GLA_DV

LANES = 128
MXU_WIDTH = 256
VMEM_LIMIT_BYTES = 60 * 1024 * 1024

GLA_CHUNK = 128
GLA_SUB = 8
GLA_LEVELS = (64, 32, 16, 8)
LOG2E = 1.4426950408889634
GLA_HEADS_PER_STEP = 4

P_QB, P_KB, P_VB, P_RB, P_GA, P_GB, P_VA = 0, 1024, 2048, 4096, 6144, 8192, 10240
P_WIDTH = 10752
QK_WIDTH = ATTN_WIDTH + KV_WIDTH
QK_GROUP = 512
STAGE_COLS = 512


def _params(*sem):
    return pltpu.CompilerParams(dimension_semantics=sem, vmem_limit_bytes=VMEM_LIMIT_BYTES)


def _sigmoid(x):
    return 1.0 / (1.0 + jnp.exp(-x))


def _permute_cast_kernel(src_ref, a_ref, o_ref):
    o_ref[...] = a_ref[...].astype(o_ref.dtype)


def _permute_cast_cols(w, src_blocks, bw):
    kdim = w.shape[0]
    n = len(src_blocks)
    src = jnp.asarray(np.asarray(src_blocks, np.int32))
    return pl.pallas_call(
        _permute_cast_kernel,
        grid_spec=pltpu.PrefetchScalarGridSpec(
            num_scalar_prefetch=1,
            grid=(n,),
            in_specs=[pl.BlockSpec((kdim, bw), lambda j, s: (0, s[j]))],
            out_specs=pl.BlockSpec((kdim, bw), lambda j, s: (0, j)),
        ),
        out_shape=jax.ShapeDtypeStruct((kdim, n * bw), BF16),
        compiler_params=_params("arbitrary"),
        name="permute_cast",
    )(src, w)


def _transpose_cast_kernel(start_ref, a_ref, o_ref):
    o_ref[...] = a_ref[...].T.astype(o_ref.dtype)


def _transpose_cast_rows(wt, row_starts, bw):
    kdim = wt.shape[1]
    n = len(row_starts)
    sub = 8
    assert all(r % sub == 0 for r in row_starts)
    starts = jnp.asarray(np.asarray(row_starts, np.int32) // sub)
    return pl.pallas_call(
        _transpose_cast_kernel,
        grid_spec=pltpu.PrefetchScalarGridSpec(
            num_scalar_prefetch=1,
            grid=(n,),
            in_specs=[pl.BlockSpec((pl.Element(bw), pl.Element(kdim)), lambda j, s: (s[j] * sub, 0))],
            out_specs=pl.BlockSpec((kdim, bw), lambda j, s: (0, j)),
        ),
        out_shape=jax.ShapeDtypeStruct((kdim, n * bw), BF16),
        compiler_params=_params("arbitrary"),
        name="transpose_cast",
    )(starts, wt)


def _mod_kernel(a_ref, w_ref, b_ref, o_ref):
    a = a_ref[...]
    s = (a * _sigmoid(a)).astype(BF16)
    o_ref[...] = jnp.dot(s, w_ref[...].astype(BF16), preferred_element_type=F32) + b_ref[...]


def _modulation(cc, w_mod, b_mod):
    d, n = w_mod.shape
    tn = 1024
    return pl.pallas_call(
        _mod_kernel,
        grid=(n // tn,),
        in_specs=[pl.BlockSpec((8, d), lambda j: (0, 0)),
                  pl.BlockSpec((d, tn), lambda j: (0, j)),
                  pl.BlockSpec((1, tn), lambda j: (0, j))],
        out_specs=pl.BlockSpec((8, tn), lambda j: (0, j)),
        out_shape=jax.ShapeDtypeStruct((8, n), F32),
        compiler_params=_params("parallel"),
        name="modulation",
    )(cc, w_mod, b_mod.reshape(1, n))


def _norm_mm_kernel(*refs, mode, rope, has_lr, tn):
    it = iter(refs)
    x_ref, shift_ref, scale_ref, gain_ref, w_ref, cs_ref = [next(it) for _ in range(6)]
    cos_ref = sin_ref = wlr_ref = lr_ref = seg_ref = perm_ref = None
    if mode == "qk":
        seg_ref, perm_ref = next(it), next(it)
    if rope:
        cos_ref, sin_ref = next(it), next(it)
    if has_lr:
        wlr_ref = next(it)
    o_ref = next(it)
    if has_lr:
        lr_ref = next(it)
    h_ref = next(it)

    @pl.when(pl.program_id(1) == 0)
    def _():
        x = x_ref[...]
        ms = jnp.mean(x * x, axis=-1, keepdims=True)
        y = x * lax.rsqrt(ms + EPS) * gain_ref[...]
        h = (y * (1.0 + scale_ref[0]) + shift_ref[0]).astype(BF16)
        h_ref[...] = h
        if has_lr:
            lr_ref[...] = jnp.dot(h, wlr_ref[...], preferred_element_type=F32)

    if mode == "plain":
        acc = jnp.dot(h_ref[...], w_ref[...], preferred_element_type=F32)
        o_ref[...] = (acc * cs_ref[...]).astype(o_ref.dtype)
    else:
        for gb in range(tn // QK_GROUP):
            big = jnp.dot(h_ref[...], w_ref[:, gb * QK_GROUP:(gb + 1) * QK_GROUP],
                          preferred_element_type=F32)
            for cb in range(QK_GROUP // MXU_WIDTH):
                c0 = gb * QK_GROUP + cb * MXU_WIDTH
                acc = big[:, cb * MXU_WIDTH:(cb + 1) * MXU_WIDTH]
                ss = jnp.dot((acc * acc).astype(BF16), seg_ref[...], preferred_element_type=F32)
                an = acc * lax.rsqrt(ss * (1.0 / HEAD_DIM) + EPS) * cs_ref[:, c0:c0 + MXU_WIDTH]
                if rope:
                    rot = jnp.dot(an.astype(BF16), perm_ref[...], preferred_element_type=F32)
                for hh in range(MXU_WIDTH // HEAD_DIM):
                    hs = slice(hh * HEAD_DIM, (hh + 1) * HEAD_DIM)
                    a = an[:, hs]
                    if rope:
                        a = a * cos_ref[...] + rot[:, hs] * sin_ref[...]
                    o_ref[:, c0 + hh * HEAD_DIM:c0 + (hh + 1) * HEAD_DIM] = a.astype(o_ref.dtype)


def _norm_matmul(x2, shift, scale, gain, w, colvec, *, mode, rows_per_batch, cos2=None, sin2=None,
                 w_lr=None, tm=1024, tn=512):
    m, d = x2.shape
    n = w.shape[1]
    tm = min(tm, rows_per_batch)
    tpb = rows_per_batch // tm
    rope = cos2 is not None
    has_lr = w_lr is not None
    in_specs = [pl.BlockSpec((tm, d), lambda i, j: (i, 0)),
                pl.BlockSpec((1, 1, d), lambda i, j: (i // tpb, 0, 0)),
                pl.BlockSpec((1, 1, d), lambda i, j: (i // tpb, 0, 0)),
                pl.BlockSpec((1, d), lambda i, j: (0, 0)),
                pl.BlockSpec((d, tn), lambda i, j: (0, j)),
                pl.BlockSpec((1, tn), lambda i, j: (0, j))]
    args = [x2, shift, scale, gain.reshape(1, d), w, colvec.reshape(1, n)]
    if mode == "qk":
        lane = np.arange(MXU_WIDTH)
        same_head = lane[:, None] // HEAD_DIM == lane[None, :] // HEAD_DIM
        rolled = (lane[:, None] % HEAD_DIM) == ((lane[None, :] - HEAD_DIM // 2) % HEAD_DIM)
        in_specs += [pl.BlockSpec((MXU_WIDTH, MXU_WIDTH), lambda i, j: (0, 0))] * 2
        args += [jnp.asarray(same_head, BF16), jnp.asarray(same_head & rolled, BF16)]
    if rope:
        in_specs += [pl.BlockSpec((tm, HEAD_DIM), lambda i, j: (i % tpb, 0))] * 2
        args += [cos2, sin2]
    out_shape = [jax.ShapeDtypeStruct((m, n), BF16)]
    out_specs = [pl.BlockSpec((tm, tn), lambda i, j: (i, j))]
    if has_lr:
        in_specs.append(pl.BlockSpec((d, LANES), lambda i, j: (0, 0)))
        args.append(w_lr)
        out_shape.append(jax.ShapeDtypeStruct((m, LANES), F32))
        out_specs.append(pl.BlockSpec((tm, LANES), lambda i, j: (i, 0)))
    res = pl.pallas_call(
        functools.partial(_norm_mm_kernel, mode=mode, rope=rope, has_lr=has_lr, tn=tn),
        grid=(m // tm, n // tn),
        in_specs=in_specs,
        out_specs=out_specs,
        out_shape=out_shape,
        scratch_shapes=[pltpu.VMEM((tm, d), BF16)],
        compiler_params=_params("parallel", "arbitrary"),
        name="norm_matmul_" + mode,
    )(*args)
    return res if has_lr else res[0]


ATTN_Q_BLOCKS = 2


def _attn_kernel(sink_ref, q_ref, *refs, nblk):
    nq = ATTN_Q_BLOCKS
    k_refs, kx_ref = refs[:nq + 2], refs[nq + 2]
    v_refs, vx_ref = refs[nq + 3:2 * nq + 5], refs[2 * nq + 5]
    o_ref = refs[2 * nq + 6]
    i = pl.program_id(1)
    blk = HEAD_DIM
    dn = (((1,), (1,)), ((), ()))
    row = lax.broadcasted_iota(jnp.int32, (blk, blk), 0)
    col = lax.broadcasted_iota(jnp.int32, (blk, blk), 1)
    ninf = jnp.float32(-jnp.inf)
    tri_p = jnp.concatenate([jnp.where(col >= row, 0.0, ninf)] * Q_PER_KV, axis=0)
    tri_n = jnp.concatenate([jnp.where(col <= row, 0.0, ninf)] * Q_PER_KV, axis=0)
    n_ctx = kx_ref.shape[0] // blk

    for a in range(nq):
        gblk = i * nq + a
        bias_p = tri_p + jnp.where(gblk > 0, 0.0, ninf)
        bias_n = tri_n + jnp.where(gblk < nblk - 1, 0.0, ninf)
        rows = slice(a * blk, (a + 1) * blk)
        for h in range(N_KV_HEADS):
            hs = slice(h * HEAD_DIM, (h + 1) * HEAD_DIM)
            qs = jnp.concatenate(
                [q_ref[rows, (h * Q_PER_KV + g) * HEAD_DIM:(h * Q_PER_KV + g + 1) * HEAD_DIM]
                 for g in range(Q_PER_KV)], axis=0)

            def scores(k_ref):
                return lax.dot_general(qs, k_ref[:, hs], dn, preferred_element_type=F32)

            s_x = scores(kx_ref)
            pieces = [scores(k_refs[a]) + bias_p, scores(k_refs[a + 1]), scores(k_refs[a + 2]) + bias_n]
            pieces += [s_x[:, j * blk:(j + 1) * blk] for j in range(n_ctx)]
            sink = jnp.concatenate(
                [jnp.full((blk, 1), sink_ref[h * Q_PER_KV + g] * LOG2E, F32) for g in range(Q_PER_KV)],
                axis=0)
            mx = pieces[0]
            for s in pieces[1:]:
                mx = jnp.maximum(mx, s)
            m = jnp.maximum(jnp.max(mx, axis=-1, keepdims=True), sink)
            probs = [jnp.exp2(s - m) for s in pieces]
            psum = probs[0]
            for pr in probs[1:]:
                psum = psum + pr
            denom = jnp.exp2(sink - m) + jnp.sum(psum, axis=-1, keepdims=True)
            p_x = jnp.concatenate(probs[3:], axis=1) if n_ctx > 1 else probs[3]
            o = (jnp.dot(probs[0].astype(BF16), v_refs[a][:, hs], preferred_element_type=F32)
                 + jnp.dot(probs[1].astype(BF16), v_refs[a + 1][:, hs], preferred_element_type=F32)
                 + jnp.dot(probs[2].astype(BF16), v_refs[a + 2][:, hs], preferred_element_type=F32)
                 + jnp.dot(p_x.astype(BF16), vx_ref[:, hs], preferred_element_type=F32))
            o = o / denom
            for g in range(Q_PER_KV):
                c0 = (h * Q_PER_KV + g) * HEAD_DIM
                o_ref[rows, c0:c0 + HEAD_DIM] = o[g * blk:(g + 1) * blk].astype(o_ref.dtype)


def _attention(qk, p, qk_c, p_c, sink, batch, seq, ctx_len):
    blk = HEAD_DIM
    nblk = seq // blk
    nq = ATTN_Q_BLOCKS
    nstep = nblk // nq
    kcol = ATTN_WIDTH // KV_WIDTH
    vcol = P_VA // KV_WIDTH

    def kv_specs(col0):
        def spec(shift):
            return pl.BlockSpec(
                (blk, KV_WIDTH), lambda b, i: (b * nblk + jnp.clip(i * nq + shift, 0, nblk - 1), col0))
        return [spec(shift) for shift in range(-1, nq + 1)]

    in_specs = ([pl.BlockSpec(memory_space=pltpu.SMEM),
                 pl.BlockSpec((nq * blk, ATTN_WIDTH), lambda b, i: (b * nstep + i, 0))]
                + kv_specs(kcol) + [pl.BlockSpec((ctx_len, KV_WIDTH), lambda b, i: (b, kcol))]
                + kv_specs(vcol) + [pl.BlockSpec((ctx_len, KV_WIDTH), lambda b, i: (b, vcol))])
    return pl.pallas_call(
        functools.partial(_attn_kernel, nblk=nblk),
        grid=(batch, nstep),
        in_specs=in_specs,
        out_specs=pl.BlockSpec((nq * blk, ATTN_WIDTH), lambda b, i: (b * nstep + i, 0)),
        out_shape=jax.ShapeDtypeStruct((batch * seq, ATTN_WIDTH), BF16),
        compiler_params=_params("parallel", "arbitrary"),
        name="window_attention",
    )(sink, qk, *([qk] * (nq + 2)), qk_c, *([p] * (nq + 2)), p_c)


def _gla_masks():
    c = GLA_CHUNK
    t = np.arange(c)[:, None]
    s = np.arange(c)[None, :]
    fwd = []
    for w in GLA_LEVELS:
        fwd.append((t // (2 * w) == s // (2 * w)) & (t % (2 * w) >= w) & (s % (2 * w) < w))
    fwd.append((t // GLA_SUB == s // GLA_SUB) & (s <= t))
    fwd.append(s <= t)
    fwd = np.stack(fwd).astype(np.float32)
    return np.stack([fwd, np.transpose(fwd, (0, 2, 1))])


def _gla_diag_select():
    c, sb = GLA_CHUNK, GLA_SUB
    sel = np.zeros((2, c * sb, c), np.float32)
    for i in range(c // sb):
        for dl in range(sb):
            for t in range(sb):
                s = (t - dl) % sb
                row = (i * sb + dl) * sb + t
                sel[0, row, i * sb + s] = float(s <= t)
                sel[1, row, i * sb + s] = float(s >= t)
    return sel


def _gla_chunk(q_ref, k_ref, v_ref, lr_parts, wg_ref, bg_ref, mask_ref, sel_ref, st_ref, w_ref, b_ref, o_ref,
               reverse, hp):
    c = GLA_CHUNK
    nl = len(GLA_LEVELS)
    ksl = slice(hp * GLA_DK, (hp + 1) * GLA_DK)
    vsl = slice(hp * GLA_DV, (hp + 1) * GLA_DV)
    q = q_ref[:, ksl].astype(F32)
    kb = k_ref[:, ksl]
    k = kb.astype(F32)
    v = v_ref[:, vsl]
    lr_hi, lr_lo = lr_parts
    wg_hi, wg_lo = wg_ref[0, :, ksl], wg_ref[1, :, ksl]
    xg = (jnp.dot(lr_hi, wg_hi, preferred_element_type=F32) + jnp.dot(lr_hi, wg_lo, preferred_element_type=F32)
          + jnp.dot(lr_lo, wg_hi, preferred_element_type=F32)) + bg_ref[:, ksl]
    yield
    g = (jnp.minimum(xg, 0.0) - jnp.log1p(jnp.exp(-jnp.abs(xg)))) * (LOG2E / GLA_GATE_NORM)

    tri = mask_ref[nl + 1].astype(BF16)
    g_hi = g.astype(BF16)
    r1 = g - g_hi.astype(F32)
    g_mid = r1.astype(BF16)
    g_lo = (r1 - g_mid.astype(F32)).astype(BF16)
    yield
    b = (jnp.dot(tri, g_hi, preferred_element_type=F32) + jnp.dot(tri, g_mid, preferred_element_type=F32)
         + jnp.dot(tri, g_lo, preferred_element_type=F32))
    b_ref[...] = b
    yield

    dn_nt = (((1,), (1,)), ((), ()))
    a_mat = None
    for lvl, w in enumerate(GLA_LEVELS):
        zero = jnp.zeros((w, GLA_DK), F32)
        qparts, kparts = [], []
        for mblk in range(c // (2 * w)):
            lo, mid, hi = mblk * 2 * w, mblk * 2 * w + w, mblk * 2 * w + 2 * w
            if reverse:
                r = b[mid:mid + 1, :]
                qparts += [q[lo:mid] * jnp.exp2(b[lo:mid] - r), zero]
                kparts += [zero, k[mid:hi] * jnp.exp2(r - b[mid:hi])]
            else:
                r = b[mid - 1:mid, :]
                qparts += [zero, q[mid:hi] * jnp.exp2(b[mid:hi] - r)]
                kparts += [k[lo:mid] * jnp.exp2(r - b[lo:mid]), zero]
        qn = jnp.concatenate(qparts, axis=0).astype(BF16)
        kn = jnp.concatenate(kparts, axis=0).astype(BF16)
        term = lax.dot_general(qn, kn, dn_nt, preferred_element_type=F32)
        if 2 * w < c:
            term = term * mask_ref[lvl]
        a_mat = term if a_mat is None else a_mat + term
        yield

    for i in range(c // GLA_SUB):
        r0 = i * GLA_SUB
        qi = q[r0:r0 + GLA_SUB]
        bi = b[r0:r0 + GLA_SUB]
        ws = [qi] + [qi * jnp.exp2(bi - pltpu.roll(bi, dl, axis=0)) for dl in range(1, GLA_SUB)]
        for dl in range(0, GLA_SUB, 2):
            w_ref[pl.ds((r0 + dl) * GLA_SUB, 2 * GLA_SUB), :] = jnp.concatenate(
                [ws[dl], ws[dl + 1]], axis=0).astype(BF16)
        yield
    red = lax.dot_general(w_ref[...], kb, dn_nt, preferred_element_type=F32)
    yield
    rows = []
    for i in range(c // GLA_SUB):
        blk = jnp.zeros((GLA_SUB, c), F32)
        for dl in range(GLA_SUB):
            rr = slice((i * GLA_SUB + dl) * GLA_SUB, (i * GLA_SUB + dl + 1) * GLA_SUB)
            blk = jnp.where(sel_ref[rr, :] > 0.5, red[rr], blk)
        rows.append(blk)
        if i % 4 == 3:
            yield
    a_mat = a_mat + jnp.concatenate(rows, axis=0)

    st = st_ref[...]
    qe = (q * jnp.exp2(b)).astype(BF16)
    o = lax.dot_general(qe, st.astype(BF16), dn_nt, preferred_element_type=F32)
    yield
    o = o + jnp.dot(a_mat.astype(BF16), v, preferred_element_type=F32)
    o_ref[:, vsl] = o.astype(o_ref.dtype)
    yield

    r_end = 0 if reverse else c - 1
    b_end = b[r_end:r_end + 1, :]
    ke = (k * jnp.exp2(b_end - b)).astype(BF16)
    upd = lax.dot_general(v, ke, (((0,), (0,)), ((), ())), preferred_element_type=F32)
    yield
    st_ref[...] = st * jnp.exp2(b_end) + upd


def _gla_kernel(qf_ref, kf_ref, vf_ref, lrf_ref, qb_ref, kb_ref, vb_ref, lrb_ref,
                wgf_ref, bgf_ref, wgb_ref, bgb_ref, mask_ref, sel_ref, s0f_ref, s0b_ref,
                of_ref, ob_ref, sf_ref, sb_ref, stf_ref, stb_ref, wf_ref, wb_ref, bf_ref, bb_ref):
    cidx = pl.program_id(2)

    @pl.when(cidx == 0)
    def _():
        stf_ref[...] = s0f_ref[0]
        stb_ref[...] = s0b_ref[0]

    def hi_lo(ref):
        x = ref[...]
        hi = x.astype(BF16)
        return hi, (x - hi.astype(F32)).astype(BF16)

    lrf, lrb = hi_lo(lrf_ref), hi_lo(lrb_ref)
    chains = []
    for hp in range(GLA_HEADS_PER_STEP):
        chains.append(_gla_chunk(qf_ref, kf_ref, vf_ref, lrf, wgf_ref, bgf_ref, mask_ref.at[0], sel_ref.at[0],
                                 stf_ref.at[hp], wf_ref.at[hp], bf_ref.at[hp], of_ref, False, hp))
        chains.append(_gla_chunk(qb_ref, kb_ref, vb_ref, lrb, wgb_ref, bgb_ref, mask_ref.at[1], sel_ref.at[1],
                                 stb_ref.at[hp], wb_ref.at[hp], bb_ref.at[hp], ob_ref, True, hp))
    while chains:
        chains = [ch for ch in chains if next(ch, True) is None]

    @pl.when(cidx == pl.num_programs(2) - 1)
    def _():
        sf_ref[0] = stf_ref[...]
        sb_ref[0] = stb_ref[...]


def _gla(p, lr, wgf, bgf, wgb, bgb, s0f, s0b, batch, seq):
    c = GLA_CHUNK
    nc = seq // c
    hps = GLA_HEADS_PER_STEP
    kw, vw = hps * GLA_DK, hps * GLA_DV
    kcol = P_KB // kw
    vcol = P_VB // vw

    def fwd(b, h, i):
        return b * nc + i

    def bwd(b, h, i):
        return b * nc + (nc - 1 - i)

    def data_specs(rowf):
        return [pl.BlockSpec((c, kw), lambda b, h, i: (rowf(b, h, i), h)),
                pl.BlockSpec((c, kw), lambda b, h, i: (rowf(b, h, i), kcol + h)),
                pl.BlockSpec((c, vw), lambda b, h, i: (rowf(b, h, i), vcol + h)),
                pl.BlockSpec((c, LANES), lambda b, h, i: (rowf(b, h, i), 0))]

    gate_specs = [pl.BlockSpec((2, LANES, kw), lambda b, h, i: (0, 0, h)),
                  pl.BlockSpec((1, kw), lambda b, h, i: (0, h))]
    state_spec = pl.BlockSpec((1, hps, GLA_DV, GLA_DK), lambda b, h, i: (b, h, 0, 0))
    masks = jnp.asarray(_gla_masks())
    select = jnp.asarray(_gla_diag_select())
    in_specs = (data_specs(fwd) + data_specs(bwd) + gate_specs + gate_specs
                + [pl.BlockSpec(masks.shape, lambda b, h, i: (0, 0, 0, 0)),
                   pl.BlockSpec(select.shape, lambda b, h, i: (0, 0, 0)), state_spec, state_spec])
    out_specs = [pl.BlockSpec((c, vw), lambda b, h, i: (fwd(b, h, i), h)),
                 pl.BlockSpec((c, vw), lambda b, h, i: (bwd(b, h, i), h)),
                 state_spec, state_spec]
    out_shape = [jax.ShapeDtypeStruct((batch * seq, GLA_V_WIDTH), BF16)] * 2 + [
        jax.ShapeDtypeStruct((batch, GLA_HEADS, GLA_DV, GLA_DK), F32)] * 2
    return pl.pallas_call(
        _gla_kernel,
        grid=(batch, GLA_HEADS // hps, nc),
        in_specs=in_specs,
        out_specs=out_specs,
        out_shape=out_shape,
        scratch_shapes=[pltpu.VMEM((hps, GLA_DV, GLA_DK), F32), pltpu.VMEM((hps, GLA_DV, GLA_DK), F32),
                        pltpu.VMEM((hps, c * GLA_SUB, GLA_DK), BF16),
                        pltpu.VMEM((hps, c * GLA_SUB, GLA_DK), BF16),
                        pltpu.VMEM((hps, c, GLA_DK), F32), pltpu.VMEM((hps, c, GLA_DK), F32)],
        compiler_params=_params("parallel", "parallel", "arbitrary"),
        name="gla_scan",
    )(p, p, p, lr, p, p, p, lr, wgf, bgf, wgb, bgb, masks, select, s0f, s0b)


def _merge_kernel(oa_ref, of_ref, ob_ref, rb_ref, ga_ref, gb_ref, gn_ref, wa_ref, wg_ref, z_ref):
    og = of_ref[...].astype(F32) + ob_ref[...].astype(F32)
    parts = []
    for hh in range(GLA_HEADS):
        a = og[:, hh * GLA_DV:(hh + 1) * GLA_DV]
        ms = jnp.mean(a * a, axis=-1, keepdims=True)
        parts.append(a * lax.rsqrt(ms + EPS) * gn_ref[...])
    n = jnp.concatenate(parts, axis=1)
    rb = rb_ref[...].astype(F32)
    n = (n * (rb * _sigmoid(rb))).astype(BF16)
    y_gla = jnp.dot(n, wg_ref[...], preferred_element_type=F32)
    y_att = jnp.dot(oa_ref[...], wa_ref[...], preferred_element_type=F32)
    z = _sigmoid(ga_ref[...].astype(F32)) * y_att + _sigmoid(gb_ref[...].astype(F32)) * y_gla
    z_ref[...] = z.astype(z_ref.dtype)


def _merge(o_attn, o_f, o_b, p, gla_norm, w_attn_o, w_gla_o, tm=256):
    m, d = o_attn.shape
    row = lambda i: (i, 0)
    const = lambda i: (0, 0)
    wspec = pl.BlockSpec((d, d), const, pipeline_mode=pl.Buffered(1))
    return pl.pallas_call(
        _merge_kernel,
        grid=(m // tm,),
        in_specs=[pl.BlockSpec((tm, d), row), pl.BlockSpec((tm, d), row), pl.BlockSpec((tm, d), row),
                  pl.BlockSpec((tm, d), lambda i: (i, P_RB // d)),
                  pl.BlockSpec((tm, d), lambda i: (i, P_GA // d)),
                  pl.BlockSpec((tm, d), lambda i: (i, P_GB // d)),
                  pl.BlockSpec((1, GLA_DV), const), wspec, wspec],
        out_specs=pl.BlockSpec((tm, d), row),
        out_shape=jax.ShapeDtypeStruct((m, d), BF16),
        compiler_params=_params("parallel"),
        name="merge_gates",
    )(o_attn, o_f, o_b, p, p, p, gla_norm.reshape(1, GLA_DV), w_attn_o, w_gla_o)


def _out_proj_kernel(z_ref, w_ref, x_ref, gate_ref, o_ref):
    y = jnp.dot(z_ref[...], w_ref[...], preferred_element_type=F32)
    o_ref[...] = x_ref[...] + gate_ref[0] * y


def _out_proj(z, w_out, x2, gate, rows_per_batch, tm=512):
    m, d = x2.shape
    tm = min(tm, rows_per_batch)
    tpb = rows_per_batch // tm
    return pl.pallas_call(
        _out_proj_kernel,
        grid=(m // tm,),
        in_specs=[pl.BlockSpec((tm, d), lambda i: (i, 0)),
                  pl.BlockSpec((d, d), lambda i: (0, 0), pipeline_mode=pl.Buffered(1)),
                  pl.BlockSpec((tm, d), lambda i: (i, 0)),
                  pl.BlockSpec((1, 1, d), lambda i: (i // tpb, 0, 0))],
        out_specs=pl.BlockSpec((tm, d), lambda i: (i, 0)),
        out_shape=jax.ShapeDtypeStruct((m, d), F32),
        compiler_params=_params("parallel"),
        name="out_proj_residual",
    )(z, w_out, x2, gate)


FFN_SUB = 256
FFN_SUBS = 2
FFN_HALO = 16
FFN_OUT_COLS = 512


def _ffn_kernel(x_ref, xp_ref, xn_ref, shift_ref, scale_ref, gate_ref, gain_ref, wu_ref, cw_ref, cb_ref,
                wd_ref, o_ref, h_ref, *, tpb):
    i = pl.program_id(0)
    kk = pl.program_id(1)
    tm = x_ref.shape[0]
    d = x_ref.shape[1]
    ext = tm + 2 * FFN_HALO

    def modnorm(x):
        ms = jnp.mean(x * x, axis=-1, keepdims=True)
        y = x * lax.rsqrt(ms + EPS) * gain_ref[...]
        return y * (1.0 + scale_ref[0]) + shift_ref[0]

    @pl.when(kk == 0)
    def _():
        keep_prev = jnp.where((i % tpb) == 0, 0.0, 1.0)
        keep_next = jnp.where((i % tpb) == tpb - 1, 0.0, 1.0)
        h_ref[0:FFN_HALO, :] = (modnorm(xp_ref[...]) * keep_prev).astype(BF16)
        h_ref[FFN_HALO:FFN_HALO + tm, :] = modnorm(x_ref[...]).astype(BF16)
        h_ref[FFN_HALO + tm:ext, :] = (modnorm(xn_ref[...]) * keep_next).astype(BF16)
        o_ref[...] = jnp.zeros_like(o_ref)

    w2 = 2 * FFN_SUB
    us = [jnp.dot(h_ref[...], wu_ref[:, s * w2:(s + 1) * w2], preferred_element_type=F32)
          for s in range(FFN_SUBS)]
    for s in range(FFN_SUBS):
        u = us[s]
        cols = slice(s * w2, (s + 1) * w2)
        rows = slice(FFN_HALO, FFN_HALO + tm)
        conv = (cw_ref[0:1, cols] * pltpu.roll(u, 1, axis=0)[rows]
                + cw_ref[1:2, cols] * u[rows]
                + cw_ref[2:3, cols] * pltpu.roll(u, ext - 1, axis=0)[rows] + cb_ref[:, cols])
        a = conv[:, :FFN_SUB]
        act = (a * _sigmoid(a) * conv[:, FFN_SUB:]).astype(BF16)
        for nb in range(d // FFN_OUT_COLS):
            oc = slice(nb * FFN_OUT_COLS, (nb + 1) * FFN_OUT_COLS)
            o_ref[:, oc] += jnp.dot(act, wd_ref[s * FFN_SUB:(s + 1) * FFN_SUB, oc],
                                    preferred_element_type=F32)

    @pl.when(kk == pl.num_programs(1) - 1)
    def _():
        o_ref[...] = x_ref[...] + gate_ref[0] * o_ref[...]


def _conv_ffn(x2, shift, scale, gate, gain, w_up_r, conv_w_r, conv_b_r, w_down, rows_per_batch, tm=1024):
    m, d = x2.shape
    dff = w_down.shape[0]
    tm = min(tm, rows_per_batch)
    tpb = rows_per_batch // tm
    hb = tm // FFN_HALO
    nhalo = m // FFN_HALO
    tk = FFN_SUB * FFN_SUBS
    nk = dff // tk
    once = pl.Buffered(1)
    return pl.pallas_call(
        functools.partial(_ffn_kernel, tpb=tpb),
        grid=(m // tm, nk),
        in_specs=[pl.BlockSpec((tm, d), lambda i, k: (i, 0), pipeline_mode=once),
                  pl.BlockSpec((FFN_HALO, d), lambda i, k: (jnp.maximum(i * hb - 1, 0), 0)),
                  pl.BlockSpec((FFN_HALO, d), lambda i, k: (jnp.minimum((i + 1) * hb, nhalo - 1), 0)),
                  pl.BlockSpec((1, 1, d), lambda i, k: (i // tpb, 0, 0)),
                  pl.BlockSpec((1, 1, d), lambda i, k: (i // tpb, 0, 0)),
                  pl.BlockSpec((1, 1, d), lambda i, k: (i // tpb, 0, 0)),
                  pl.BlockSpec((1, d), lambda i, k: (0, 0)),
                  pl.BlockSpec((d, 2 * tk), lambda i, k: (0, k)),
                  pl.BlockSpec((3, 2 * tk), lambda i, k: (0, k)),
                  pl.BlockSpec((1, 2 * tk), lambda i, k: (0, k)),
                  pl.BlockSpec((tk, d), lambda i, k: (k, 0))],
        out_specs=pl.BlockSpec((tm, d), lambda i, k: (i, 0)),
        out_shape=jax.ShapeDtypeStruct((m, d), F32),
        scratch_shapes=[pltpu.VMEM((tm + 2 * FFN_HALO, d), BF16)],
        compiler_params=_params("parallel", "arbitrary"),
        name="conv_ffn",
    )(x2, x2, x2, shift, scale, gate, gain.reshape(1, d), w_up_r, conv_w_r, conv_b_r, w_down)


def _regroup_ffn_cols(a):
    lead = a.shape[:-1]
    nblk = D_FF // FFN_SUB
    a = a.reshape(*lead, 2, nblk, FFN_SUB)
    return jnp.swapaxes(a, -3, -2).reshape(*lead, 2 * D_FF)


def _rope_tables(n):
    rows = n // GRID_W
    row = jnp.repeat(jnp.arange(rows), GRID_W)
    col = jnp.tile(jnp.arange(GRID_W), rows)
    n_freq = HEAD_DIM // 4
    inv = ROPE_THETA ** (-jnp.arange(n_freq, dtype=F32) / n_freq)
    ang = jnp.concatenate([row[:, None] * inv, col[:, None] * inv], axis=-1)
    cos, sin = jnp.cos(ang), jnp.sin(ang)
    return jnp.concatenate([cos, cos], axis=-1), jnp.concatenate([-sin, sin], axis=-1)


def _split_w_in(w_in):
    names = ("qa", "ka", "va", "qb", "kb", "vb", "rb", "lrf", "lrb", "ga", "gb")
    splits = (ATTN_WIDTH, KV_WIDTH, KV_WIDTH, GLA_K_WIDTH, GLA_K_WIDTH, GLA_V_WIDTH, GLA_V_WIDTH,
              GLA_LOWRANK, GLA_LOWRANK, D_MODEL, D_MODEL)
    start = dict(zip(names, np.cumsum((0,) + splits[:-1]).tolist()))
    width = dict(zip(names, splits))
    bw = STAGE_COLS
    wt = jnp.swapaxes(w_in, 0, 1)

    def starts(group):
        out = []
        for nm in group:
            assert start[nm] % 8 == 0 and width[nm] % bw == 0
            out += [start[nm] + t * bw for t in range(width[nm] // bw)]
        return out

    w_qk = _transpose_cast_rows(wt, starts(("qa", "ka")), bw)
    w_p = _transpose_cast_rows(wt, starts(("qb", "kb", "vb", "rb", "ga", "gb", "va")), bw)
    lr0, nlr = start["lrf"], 2 * GLA_LOWRANK
    pad = jnp.zeros((w_in.shape[0], LANES - nlr), w_in.dtype)
    w_lr = jnp.concatenate([jnp.swapaxes(wt[lr0:lr0 + nlr], 0, 1), pad], axis=-1).astype(BF16)
    return w_qk, w_p, w_lr


def kernel(x, c, ctx, c_ctx, w_mod, b_mod, g_mix, w_in, q_norm, k_norm, attn_sink, w_gate_f, b_gate_f,
           w_gate_b, b_gate_b, gla_norm, w_attn_o, w_gla_o, w_out, g_ffn, w_up, conv_w, conv_b, w_down):
    batch, seq, d = x.shape
    ctx_len = ctx.shape[1]
    assert w_mod.shape[0] == 1, "single-layer kernel"
    assert d == D_MODEL and seq % GLA_CHUNK == 0 and ctx_len % GLA_CHUNK == 0 and batch <= 7

    cc = jnp.zeros((8, d), F32).at[:batch].set(c).at[batch].set(c_ctx)
    mod = _modulation(cc, w_mod[0], b_mod[0]).reshape(8, 6, d)
    mod_x = [mod[:batch, jj][:, None, :] for jj in range(6)]
    mod_c = [mod[batch:batch + 1, jj][:, None, :] for jj in range(6)]

    w_qk, w_p, w_lr = _split_w_in(w_in[0])
    q_fold = HEAD_DIM ** -0.5 * LOG2E
    qk_norm_w = jnp.concatenate([jnp.tile(q_norm[0] * q_fold, N_Q_HEADS), jnp.tile(k_norm[0], N_KV_HEADS)])
    p_scale = jnp.ones((P_WIDTH,), F32).at[P_QB:P_QB + GLA_K_WIDTH].set(GLA_DK ** -0.5)
    cos2, sin2 = _rope_tables(seq)

    x2 = x.reshape(batch * seq, d)
    c2 = ctx.reshape(batch * ctx_len, d)
    qk, lr = _norm_matmul(x2, mod_x[0], mod_x[1], g_mix[0], w_qk, qk_norm_w, mode="qk",
                          rows_per_batch=seq, cos2=cos2, sin2=sin2, w_lr=w_lr, tm=512, tn=QK_WIDTH)
    p = _norm_matmul(x2, mod_x[0], mod_x[1], g_mix[0], w_p, p_scale, mode="plain", rows_per_batch=seq,
                     tn=1536)
    qk_c, lr_c = _norm_matmul(c2, mod_c[0], mod_c[1], g_mix[0], w_qk, qk_norm_w, mode="qk",
                              rows_per_batch=batch * ctx_len, w_lr=w_lr, tm=512, tn=QK_WIDTH)
    p_c = _norm_matmul(c2, mod_c[0], mod_c[1], g_mix[0], w_p, p_scale, mode="plain",
                       rows_per_batch=batch * ctx_len)

    o_attn = _attention(qk, p, qk_c, p_c, attn_sink[0], batch, seq, ctx_len)

    def hi_lo(w):
        hi = w.astype(BF16)
        return jnp.stack([hi, (w - hi.astype(F32)).astype(BF16)])

    wgf = hi_lo(jnp.zeros((LANES, GLA_K_WIDTH), F32).at[:GLA_LOWRANK].set(w_gate_f[0]))
    wgb = hi_lo(jnp.zeros((LANES, GLA_K_WIDTH), F32).at[GLA_LOWRANK:2 * GLA_LOWRANK].set(w_gate_b[0]))
    bgf = b_gate_f[0].reshape(1, GLA_K_WIDTH)
    bgb = b_gate_b[0].reshape(1, GLA_K_WIDTH)
    zero_state = jnp.zeros((batch, GLA_HEADS, GLA_DV, GLA_DK), F32)
    _, _, sf, sb = _gla(p_c, lr_c, wgf, bgf, wgb, bgb, zero_state, zero_state, batch, ctx_len)
    o_f, o_b, _, _ = _gla(p, lr, wgf, bgf, wgb, bgb, sf, sb, batch, seq)

    z = _merge(o_attn, o_f, o_b, p, gla_norm[0], w_attn_o[0].astype(BF16), w_gla_o[0].astype(BF16))
    x1 = _out_proj(z, w_out[0].astype(BF16), x2, mod_x[2], seq)

    nblk = D_FF // FFN_SUB
    w_up_r = _permute_cast_cols(w_up[0], [(j % 2) * nblk + j // 2 for j in range(2 * nblk)], FFN_SUB)
    out = _conv_ffn(x1, mod_x[3], mod_x[4], mod_x[5], g_ffn[0], w_up_r,
                    _regroup_ffn_cols(conv_w[0]), _regroup_ffn_cols(conv_b[0]).reshape(1, 2 * D_FF),
                    w_down[0].astype(BF16), seq)
    return out.reshape(batch, seq, d)
```

```python
import functools

import numpy as np
import jax
import jax.numpy as jnp
from jax import lax
from jax.experimental import pallas as pl
from jax.experimental.pallas import tpu as pltpu

F32 = jnp.float32
BF16 = jnp.bfloat16

D_MODEL = 2048
GRID_W = 64
HEAD_DIM = 128
N_Q_HEADS = 16
N_KV_HEADS = 4
Q_PER_KV = N_Q_HEADS // N_KV_HEADS
WINDOW = 128
ROPE_THETA = 10000.0
GLA_HEADS = 4
GLA_DK = D_MODEL // 2 // GLA_HEADS
GLA_DV = D_MODEL // GLA_HEADS
GLA_LOWRANK = 16
GLA_GATE_NORM = 16.0
D_FF = 5632
EPS = 1e-6
ATTN_WIDTH = N_Q_HEADS * HEAD_DIM
KV_WIDTH = N_KV_HEADS * HEAD_DIM
GLA_K_WIDTH = GLA_HEADS * GLA_DK
GLA_V_WIDTH = GLA_HEADS * GLA_DV

LANES = 128
MXU_WIDTH = 256
VMEM_LIMIT_BYTES = 60 * 1024 * 1024

GLA_CHUNK = 128
GLA_SUB = 8
GLA_LEVELS = (64, 32, 16, 8)
LOG2E = 1.4426950408889634
GLA_HEADS_PER_STEP = 4

P_QB, P_KB, P_VB, P_RB, P_GA, P_GB, P_VA = 0, 1024, 2048, 4096, 6144, 8192, 10240
P_WIDTH = 10752
QK_WIDTH = ATTN_WIDTH + KV_WIDTH
QK_GROUP = 512
STAGE_COLS = 512


def _params(*sem):
    return pltpu.CompilerParams(dimension_semantics=sem, vmem_limit_bytes=VMEM_LIMIT_BYTES)


def _sigmoid(x):
    return 1.0 / (1.0 + jnp.exp(-x))


def _permute_cast_kernel(src_ref, a_ref, o_ref):
    o_ref[...] = a_ref[...].astype(o_ref.dtype)


def _permute_cast_cols(w, src_blocks, bw):
    kdim = w.shape[0]
    n = len(src_blocks)
    src = jnp.asarray(np.asarray(src_blocks, np.int32))
    return pl.pallas_call(
        _permute_cast_kernel,
        grid_spec=pltpu.PrefetchScalarGridSpec(
            num_scalar_prefetch=1,
            grid=(n,),
            in_specs=[pl.BlockSpec((kdim, bw), lambda j, s: (0, s[j]))],
            out_specs=pl.BlockSpec((kdim, bw), lambda j, s: (0, j)),
        ),
        out_shape=jax.ShapeDtypeStruct((kdim, n * bw), BF16),
        compiler_params=_params("arbitrary"),
        name="permute_cast",
    )(src, w)


def _transpose_cast_kernel(start_ref, a_ref, o_ref):
    o_ref[...] = a_ref[...].T.astype(o_ref.dtype)


def _transpose_cast_rows(wt, row_starts, bw):
    kdim = wt.shape[1]
    n = len(row_starts)
    sub = 8
    assert all(r % sub == 0 for r in row_starts)
    starts = jnp.asarray(np.asarray(row_starts, np.int32) // sub)
    return pl.pallas_call(
        _transpose_cast_kernel,
        grid_spec=pltpu.PrefetchScalarGridSpec(
            num_scalar_prefetch=1,
            grid=(n,),
            in_specs=[pl.BlockSpec((pl.Element(bw), pl.Element(kdim)), lambda j, s: (s[j] * sub, 0))],
            out_specs=pl.BlockSpec((kdim, bw), lambda j, s: (0, j)),
        ),
        out_shape=jax.ShapeDtypeStruct((kdim, n * bw), BF16),
        compiler_params=_params("arbitrary"),
        name="transpose_cast",
    )(starts, wt)


def _mod_kernel(a_ref, w_ref, b_ref, o_ref):
    a = a_ref[...]
    s = (a * _sigmoid(a)).astype(BF16)
    o_ref[...] = jnp.dot(s, w_ref[...].astype(BF16), preferred_element_type=F32) + b_ref[...]


def _modulation(cc, w_mod, b_mod):
    d, n = w_mod.shape
    tn = 1024
    return pl.pallas_call(
        _mod_kernel,
        grid=(n // tn,),
        in_specs=[pl.BlockSpec((8, d), lambda j: (0, 0)),
                  pl.BlockSpec((d, tn), lambda j: (0, j)),
                  pl.BlockSpec((1, tn), lambda j: (0, j))],
        out_specs=pl.BlockSpec((8, tn), lambda j: (0, j)),
        out_shape=jax.ShapeDtypeStruct((8, n), F32),
        compiler_params=_params("parallel"),
        name="modulation",
    )(cc, w_mod, b_mod.reshape(1, n))


def _norm_mm_kernel(*refs, mode, rope, has_lr, tn):
    it = iter(refs)
    x_ref, shift_ref, scale_ref, gain_ref, w_ref, cs_ref = [next(it) for _ in range(6)]
    cos_ref = sin_ref = wlr_ref = lr_ref = seg_ref = perm_ref = None
    if mode == "qk":
        seg_ref, perm_ref = next(it), next(it)
    if rope:
        cos_ref, sin_ref = next(it), next(it)
    if has_lr:
        wlr_ref = next(it)
    o_ref = next(it)
    if has_lr:
        lr_ref = next(it)
    h_ref = next(it)

    @pl.when(pl.program_id(1) == 0)
    def _():
        x = x_ref[...]
        ms = jnp.mean(x * x, axis=-1, keepdims=True)
        y = x * lax.rsqrt(ms + EPS) * gain_ref[...]
        h = (y * (1.0 + scale_ref[0]) + shift_ref[0]).astype(BF16)
        h_ref[...] = h
        if has_lr:
            lr_ref[...] = jnp.dot(h, wlr_ref[...], preferred_element_type=F32)

    if mode == "plain":
        acc = jnp.dot(h_ref[...], w_ref[...], preferred_element_type=F32)
        o_ref[...] = (acc * cs_ref[...]).astype(o_ref.dtype)
    else:
        for gb in range(tn // QK_GROUP):
            big = jnp.dot(h_ref[...], w_ref[:, gb * QK_GROUP:(gb + 1) * QK_GROUP],
                          preferred_element_type=F32)
            for cb in range(QK_GROUP // MXU_WIDTH):
                c0 = gb * QK_GROUP + cb * MXU_WIDTH
                acc = big[:, cb * MXU_WIDTH:(cb + 1) * MXU_WIDTH]
                ss = jnp.dot((acc * acc).astype(BF16), seg_ref[...], preferred_element_type=F32)
                an = acc * lax.rsqrt(ss * (1.0 / HEAD_DIM) + EPS) * cs_ref[:, c0:c0 + MXU_WIDTH]
                if rope:
                    rot = jnp.dot(an.astype(BF16), perm_ref[...], preferred_element_type=F32)
                for hh in range(MXU_WIDTH // HEAD_DIM):
                    hs = slice(hh * HEAD_DIM, (hh + 1) * HEAD_DIM)
                    a = an[:, hs]
                    if rope:
                        a = a * cos_ref[...] + rot[:, hs] * sin_ref[...]
                    o_ref[:, c0 + hh * HEAD_DIM:c0 + (hh + 1) * HEAD_DIM] = a.astype(o_ref.dtype)


def _norm_matmul(x2, shift, scale, gain, w, colvec, *, mode, rows_per_batch, cos2=None, sin2=None,
                 w_lr=None, tm=1024, tn=512):
    m, d = x2.shape
    n = w.shape[1]
    tm = min(tm, rows_per_batch)
    tpb = rows_per_batch // tm
    rope = cos2 is not None
    has_lr = w_lr is not None
    in_specs = [pl.BlockSpec((tm, d), lambda i, j: (i, 0)),
                pl.BlockSpec((1, 1, d), lambda i, j: (i // tpb, 0, 0)),
                pl.BlockSpec((1, 1, d), lambda i, j: (i // tpb, 0, 0)),
                pl.BlockSpec((1, d), lambda i, j: (0, 0)),
                pl.BlockSpec((d, tn), lambda i, j: (0, j)),
                pl.BlockSpec((1, tn), lambda i, j: (0, j))]
    args = [x2, shift, scale, gain.reshape(1, d), w, colvec.reshape(1, n)]
    if mode == "qk":
        lane = np.arange(MXU_WIDTH)
        same_head = lane[:, None] // HEAD_DIM == lane[None, :] // HEAD_DIM
        rolled = (lane[:, None] % HEAD_DIM) == ((lane[None, :] - HEAD_DIM // 2) % HEAD_DIM)
        in_specs += [pl.BlockSpec((MXU_WIDTH, MXU_WIDTH), lambda i, j: (0, 0))] * 2
        args += [jnp.asarray(same_head, BF16), jnp.asarray(same_head & rolled, BF16)]
    if rope:
        in_specs += [pl.BlockSpec((tm, HEAD_DIM), lambda i, j: (i % tpb, 0))] * 2
        args += [cos2, sin2]
    out_shape = [jax.ShapeDtypeStruct((m, n), BF16)]
    out_specs = [pl.BlockSpec((tm, tn), lambda i, j: (i, j))]
    if has_lr:
        in_specs.append(pl.BlockSpec((d, LANES), lambda i, j: (0, 0)))
        args.append(w_lr)
        out_shape.append(jax.ShapeDtypeStruct((m, LANES), F32))
        out_specs.append(pl.BlockSpec((tm, LANES), lambda i, j: (i, 0)))
    res = pl.pallas_call(
        functools.partial(_norm_mm_kernel, mode=mode, rope=rope, has_lr=has_lr, tn=tn),
        grid=(m // tm, n // tn),
        in_specs=in_specs,
        out_specs=out_specs,
        out_shape=out_shape,
        scratch_shapes=[pltpu.VMEM((tm, d), BF16)],
        compiler_params=_params("parallel", "arbitrary"),
        name="norm_matmul_" + mode,
    )(*args)
    return res if has_lr else res[0]


ATTN_Q_BLOCKS = 2


def _attn_kernel(sink_ref, q_ref, *refs, nblk):
    nq = ATTN_Q_BLOCKS
    k_refs, kx_ref = refs[:nq + 2], refs[nq + 2]
    v_refs, vx_ref = refs[nq + 3:2 * nq + 5], refs[2 * nq + 5]
    o_ref = refs[2 * nq + 6]
    i = pl.program_id(1)
    blk = HEAD_DIM
    dn = (((1,), (1,)), ((), ()))
    row = lax.broadcasted_iota(jnp.int32, (blk, blk), 0)
    col = lax.broadcasted_iota(jnp.int32, (blk, blk), 1)
    ninf = jnp.float32(-jnp.inf)
    tri_p = jnp.concatenate([jnp.where(col >= row, 0.0, ninf)] * Q_PER_KV, axis=0)
    tri_n = jnp.concatenate([jnp.where(col <= row, 0.0, ninf)] * Q_PER_KV, axis=0)
    n_ctx = kx_ref.shape[0] // blk

    for a in range(nq):
        gblk = i * nq + a
        bias_p = tri_p + jnp.where(gblk > 0, 0.0, ninf)
        bias_n = tri_n + jnp.where(gblk < nblk - 1, 0.0, ninf)
        rows = slice(a * blk, (a + 1) * blk)
        for h in range(N_KV_HEADS):
            hs = slice(h * HEAD_DIM, (h + 1) * HEAD_DIM)
            qs = jnp.concatenate(
                [q_ref[rows, (h * Q_PER_KV + g) * HEAD_DIM:(h * Q_PER_KV + g + 1) * HEAD_DIM]
                 for g in range(Q_PER_KV)], axis=0)

            def scores(k_ref):
                return lax.dot_general(qs, k_ref[:, hs], dn, preferred_element_type=F32)

            s_x = scores(kx_ref)
            pieces = [scores(k_refs[a]) + bias_p, scores(k_refs[a + 1]), scores(k_refs[a + 2]) + bias_n]
            pieces += [s_x[:, j * blk:(j + 1) * blk] for j in range(n_ctx)]
            sink = jnp.concatenate(
                [jnp.full((blk, 1), sink_ref[h * Q_PER_KV + g] * LOG2E, F32) for g in range(Q_PER_KV)],
                axis=0)
            mx = pieces[0]
            for s in pieces[1:]:
                mx = jnp.maximum(mx, s)
            m = jnp.maximum(jnp.max(mx, axis=-1, keepdims=True), sink)
            probs = [jnp.exp2(s - m) for s in pieces]
            psum = probs[0]
            for pr in probs[1:]:
                psum = psum + pr
            denom = jnp.exp2(sink - m) + jnp.sum(psum, axis=-1, keepdims=True)
            p_x = jnp.concatenate(probs[3:], axis=1) if n_ctx > 1 else probs[3]
            o = (jnp.dot(probs[0].astype(BF16), v_refs[a][:, hs], preferred_element_type=F32)
                 + jnp.dot(probs[1].astype(BF16), v_refs[a + 1][:, hs], preferred_element_type=F32)
                 + jnp.dot(probs[2].astype(BF16), v_refs[a + 2][:, hs], preferred_element_type=F32)
                 + jnp.dot(p_x.astype(BF16), vx_ref[:, hs], preferred_element_type=F32))
            o = o / denom
            for g in range(Q_PER_KV):
                c0 = (h * Q_PER_KV + g) * HEAD_DIM
                o_ref[rows, c0:c0 + HEAD_DIM] = o[g * blk:(g + 1) * blk].astype(o_ref.dtype)


def _attention(qk, p, qk_c, p_c, sink, batch, seq, ctx_len):
    blk = HEAD_DIM
    nblk = seq // blk
    nq = ATTN_Q_BLOCKS
    nstep = nblk // nq
    kcol = ATTN_WIDTH // KV_WIDTH
    vcol = P_VA // KV_WIDTH

    def kv_specs(col0):
        def spec(shift):
            return pl.BlockSpec(
                (blk, KV_WIDTH), lambda b, i: (b * nblk + jnp.clip(i * nq + shift, 0, nblk - 1), col0))
        return [spec(shift) for shift in range(-1, nq + 1)]

    in_specs = ([pl.BlockSpec(memory_space=pltpu.SMEM),
                 pl.BlockSpec((nq * blk, ATTN_WIDTH), lambda b, i: (b * nstep + i, 0))]
                + kv_specs(kcol) + [pl.BlockSpec((ctx_len, KV_WIDTH), lambda b, i: (b, kcol))]
                + kv_specs(vcol) + [pl.BlockSpec((ctx_len, KV_WIDTH), lambda b, i: (b, vcol))])
    return pl.pallas_call(
        functools.partial(_attn_kernel, nblk=nblk),
        grid=(batch, nstep),
        in_specs=in_specs,
        out_specs=pl.BlockSpec((nq * blk, ATTN_WIDTH), lambda b, i: (b * nstep + i, 0)),
        out_shape=jax.ShapeDtypeStruct((batch * seq, ATTN_WIDTH), BF16),
        compiler_params=_params("parallel", "arbitrary"),
        name="window_attention",
    )(sink, qk, *([qk] * (nq + 2)), qk_c, *([p] * (nq + 2)), p_c)


def _gla_masks():
    c = GLA_CHUNK
    t = np.arange(c)[:, None]
    s = np.arange(c)[None, :]
    fwd = []
    for w in GLA_LEVELS:
        fwd.append((t // (2 * w) == s // (2 * w)) & (t % (2 * w) >= w) & (s % (2 * w) < w))
    fwd.append(s <= t)
    fwd = np.stack(fwd).astype(np.float32)
    return np.stack([fwd, np.transpose(fwd, (0, 2, 1))])


def _gla_diag_select():
    c, sb = GLA_CHUNK, GLA_SUB
    sel = np.zeros((2, c * sb, c), np.float32)
    for i in range(c // sb):
        for dl in range(sb):
            for t in range(sb):
                s = (t - dl) % sb
                row = (i * sb + dl) * sb + t
                sel[0, row, i * sb + s] = float(s <= t)
                sel[1, row, i * sb + s] = float(s >= t)
    return sel


def _gla_chunk(q_ref, k_ref, v_ref, lr_parts, wg_ref, bg_ref, mask_ref, sel_ref, st_ref, w_ref, o_ref,
               reverse, hp):
    c = GLA_CHUNK
    nl = len(GLA_LEVELS)
    ksl = slice(hp * GLA_DK, (hp + 1) * GLA_DK)
    vsl = slice(hp * GLA_DV, (hp + 1) * GLA_DV)
    q = q_ref[:, ksl].astype(F32)
    kb = k_ref[:, ksl]
    k = kb.astype(F32)
    v = v_ref[:, vsl]
    lr_hi, lr_lo = lr_parts
    wg_hi, wg_lo = wg_ref[0, :, ksl], wg_ref[1, :, ksl]
    xg = (jnp.dot(lr_hi, wg_hi, preferred_element_type=F32) + jnp.dot(lr_hi, wg_lo, preferred_element_type=F32)
          + jnp.dot(lr_lo, wg_hi, preferred_element_type=F32)) + bg_ref[:, ksl]
    yield
    g = (jnp.minimum(xg, 0.0) - jnp.log1p(jnp.exp(-jnp.abs(xg)))) * (LOG2E / GLA_GATE_NORM)

    tri = mask_ref[nl].astype(BF16)
    g_hi = g.astype(BF16)
    r1 = g - g_hi.astype(F32)
    g_mid = r1.astype(BF16)
    g_lo = (r1 - g_mid.astype(F32)).astype(BF16)
    yield
    b = (jnp.dot(tri, g_hi, preferred_element_type=F32) + jnp.dot(tri, g_mid, preferred_element_type=F32)
         + jnp.dot(tri, g_lo, preferred_element_type=F32))
    yield

    dn_nt = (((1,), (1,)), ((), ()))
    a_mat = None
    for lvl, w in enumerate(GLA_LEVELS):
        zero = jnp.zeros((w, GLA_DK), F32)
        qparts, kparts = [], []
        for mblk in range(c // (2 * w)):
            lo, mid, hi = mblk * 2 * w, mblk * 2 * w + w, mblk * 2 * w + 2 * w
            if reverse:
                r = b[mid:mid + 1, :]
                qparts += [q[lo:mid] * jnp.exp2(b[lo:mid] - r), zero]
                kparts += [zero, k[mid:hi] * jnp.exp2(r - b[mid:hi])]
            else:
                r = b[mid - 1:mid, :]
                qparts += [zero, q[mid:hi] * jnp.exp2(b[mid:hi] - r)]
                kparts += [k[lo:mid] * jnp.exp2(r - b[lo:mid]), zero]
        qn = jnp.concatenate(qparts, axis=0).astype(BF16)
        kn = jnp.concatenate(kparts, axis=0).astype(BF16)
        term = lax.dot_general(qn, kn, dn_nt, preferred_element_type=F32)
        if 2 * w < c:
            term = term * mask_ref[lvl]
        a_mat = term if a_mat is None else a_mat + term
        yield

    for i in range(c // GLA_SUB):
        r0 = i * GLA_SUB
        qi = q[r0:r0 + GLA_SUB]
        bi = b[r0:r0 + GLA_SUB]
        ws = [qi] + [qi * jnp.exp2(bi - pltpu.roll(bi, dl, axis=0)) for dl in range(1, GLA_SUB)]
        for dl in range(0, GLA_SUB, 2):
            w_ref[pl.ds((r0 + dl) * GLA_SUB, 2 * GLA_SUB), :] = jnp.concatenate(
                [ws[dl], ws[dl + 1]], axis=0).astype(BF16)
        yield
    red = lax.dot_general(w_ref[...], kb, dn_nt, preferred_element_type=F32)
    yield
    rows = []
    for i in range(c // GLA_SUB):
        blk = jnp.zeros((GLA_SUB, c), F32)
        for dl in range(GLA_SUB):
            rr = slice((i * GLA_SUB + dl) * GLA_SUB, (i * GLA_SUB + dl + 1) * GLA_SUB)
            blk = jnp.where(sel_ref[rr, :] > 0.5, red[rr], blk)
        rows.append(blk)
        if i % 4 == 3:
            yield
    a_mat = a_mat + jnp.concatenate(rows, axis=0)

    st = st_ref[...]
    qe = (q * jnp.exp2(b)).astype(BF16)
    o = lax.dot_general(qe, st.astype(BF16), dn_nt, preferred_element_type=F32)
    yield
    o = o + jnp.dot(a_mat.astype(BF16), v, preferred_element_type=F32)
    o_ref[:, vsl] = o.astype(o_ref.dtype)
    yield

    r_end = 0 if reverse else c - 1
    b_end = b[r_end:r_end + 1, :]
    ke = (k * jnp.exp2(b_end - b)).astype(BF16)
    upd = lax.dot_general(v, ke, (((0,), (0,)), ((), ())), preferred_element_type=F32)
    yield
    st_ref[...] = st * jnp.exp2(b_end) + upd


def _gla_kernel(qf_ref, kf_ref, vf_ref, lrf_ref, qb_ref, kb_ref, vb_ref, lrb_ref,
                wgf_ref, bgf_ref, wgb_ref, bgb_ref, mask_ref, sel_ref, s0f_ref, s0b_ref,
                of_ref, ob_ref, sf_ref, sb_ref, stf_ref, stb_ref, wf_ref, wb_ref):
    cidx = pl.program_id(2)

    @pl.when(cidx == 0)
    def _():
        stf_ref[...] = s0f_ref[0]
        stb_ref[...] = s0b_ref[0]

    def hi_lo(ref):
        x = ref[...]
        hi = x.astype(BF16)
        return hi, (x - hi.astype(F32)).astype(BF16)

    lrf, lrb = hi_lo(lrf_ref), hi_lo(lrb_ref)
    chains = []
    for hp in range(GLA_HEADS_PER_STEP):
        chains.append(_gla_chunk(qf_ref, kf_ref, vf_ref, lrf, wgf_ref, bgf_ref, mask_ref.at[0], sel_ref.at[0],
                                 stf_ref.at[hp], wf_ref.at[hp], of_ref, False, hp))
        chains.append(_gla_chunk(qb_ref, kb_ref, vb_ref, lrb, wgb_ref, bgb_ref, mask_ref.at[1], sel_ref.at[1],
                                 stb_ref.at[hp], wb_ref.at[hp], ob_ref, True, hp))
    while chains:
        chains = [ch for ch in chains if next(ch, True) is None]

    @pl.when(cidx == pl.num_programs(2) - 1)
    def _():
        sf_ref[0] = stf_ref[...]
        sb_ref[0] = stb_ref[...]


def _gla(p, lr, wgf, bgf, wgb, bgb, s0f, s0b, batch, seq):
    c = GLA_CHUNK
    nc = seq // c
    hps = GLA_HEADS_PER_STEP
    kw, vw = hps * GLA_DK, hps * GLA_DV
    kcol = P_KB // kw
    vcol = P_VB // vw

    def fwd(b, h, i):
        return b * nc + i

    def bwd(b, h, i):
        return b * nc + (nc - 1 - i)

    def data_specs(rowf):
        return [pl.BlockSpec((c, kw), lambda b, h, i: (rowf(b, h, i), h)),
                pl.BlockSpec((c, kw), lambda b, h, i: (rowf(b, h, i), kcol + h)),
                pl.BlockSpec((c, vw), lambda b, h, i: (rowf(b, h, i), vcol + h)),
                pl.BlockSpec((c, LANES), lambda b, h, i: (rowf(b, h, i), 0))]

    gate_specs = [pl.BlockSpec((2, LANES, kw), lambda b, h, i: (0, 0, h)),
                  pl.BlockSpec((1, kw), lambda b, h, i: (0, h))]
    state_spec = pl.BlockSpec((1, hps, GLA_DV, GLA_DK), lambda b, h, i: (b, h, 0, 0))
    masks = jnp.asarray(_gla_masks())
    select = jnp.asarray(_gla_diag_select())
    in_specs = (data_specs(fwd) + data_specs(bwd) + gate_specs + gate_specs
                + [pl.BlockSpec(masks.shape, lambda b, h, i: (0, 0, 0, 0)),
                   pl.BlockSpec(select.shape, lambda b, h, i: (0, 0, 0)), state_spec, state_spec])
    out_specs = [pl.BlockSpec((c, vw), lambda b, h, i: (fwd(b, h, i), h)),
                 pl.BlockSpec((c, vw), lambda b, h, i: (bwd(b, h, i), h)),
                 state_spec, state_spec]
    out_shape = [jax.ShapeDtypeStruct((batch * seq, GLA_V_WIDTH), BF16)] * 2 + [
        jax.ShapeDtypeStruct((batch, GLA_HEADS, GLA_DV, GLA_DK), F32)] * 2
    return pl.pallas_call(
        _gla_kernel,
        grid=(batch, GLA_HEADS // hps, nc),
        in_specs=in_specs,
        out_specs=out_specs,
        out_shape=out_shape,
        scratch_shapes=[pltpu.VMEM((hps, GLA_DV, GLA_DK), F32), pltpu.VMEM((hps, GLA_DV, GLA_DK), F32),
                        pltpu.VMEM((hps, c * GLA_SUB, GLA_DK), BF16),
                        pltpu.VMEM((hps, c * GLA_SUB, GLA_DK), BF16)],
        compiler_params=_params("parallel", "parallel", "arbitrary"),
        name="gla_scan",
    )(p, p, p, lr, p, p, p, lr, wgf, bgf, wgb, bgb, masks, select, s0f, s0b)


def _merge_kernel(oa_ref, of_ref, ob_ref, rb_ref, ga_ref, gb_ref, gn_ref, wa_ref, wg_ref, z_ref):
    og = of_ref[...].astype(F32) + ob_ref[...].astype(F32)
    parts = []
    for hh in range(GLA_HEADS):
        a = og[:, hh * GLA_DV:(hh + 1) * GLA_DV]
        ms = jnp.mean(a * a, axis=-1, keepdims=True)
        parts.append(a * lax.rsqrt(ms + EPS) * gn_ref[...])
    n = jnp.concatenate(parts, axis=1)
    rb = rb_ref[...].astype(F32)
    n = (n * (rb * _sigmoid(rb))).astype(BF16)
    y_gla = jnp.dot(n, wg_ref[...], preferred_element_type=F32)
    y_att = jnp.dot(oa_ref[...], wa_ref[...], preferred_element_type=F32)
    z = _sigmoid(ga_ref[...].astype(F32)) * y_att + _sigmoid(gb_ref[...].astype(F32)) * y_gla
    z_ref[...] = z.astype(z_ref.dtype)


def _merge(o_attn, o_f, o_b, p, gla_norm, w_attn_o, w_gla_o, tm=256):
    m, d = o_attn.shape
    row = lambda i: (i, 0)
    const = lambda i: (0, 0)
    wspec = pl.BlockSpec((d, d), const, pipeline_mode=pl.Buffered(1))
    return pl.pallas_call(
        _merge_kernel,
        grid=(m // tm,),
        in_specs=[pl.BlockSpec((tm, d), row), pl.BlockSpec((tm, d), row), pl.BlockSpec((tm, d), row),
                  pl.BlockSpec((tm, d), lambda i: (i, P_RB // d)),
                  pl.BlockSpec((tm, d), lambda i: (i, P_GA // d)),
                  pl.BlockSpec((tm, d), lambda i: (i, P_GB // d)),
                  pl.BlockSpec((1, GLA_DV), const), wspec, wspec],
        out_specs=pl.BlockSpec((tm, d), row),
        out_shape=jax.ShapeDtypeStruct((m, d), BF16),
        compiler_params=_params("parallel"),
        name="merge_gates",
    )(o_attn, o_f, o_b, p, p, p, gla_norm.reshape(1, GLA_DV), w_attn_o, w_gla_o)


def _out_proj_kernel(z_ref, w_ref, x_ref, gate_ref, o_ref):
    y = jnp.dot(z_ref[...], w_ref[...], preferred_element_type=F32)
    o_ref[...] = x_ref[...] + gate_ref[0] * y


def _out_proj(z, w_out, x2, gate, rows_per_batch, tm=512):
    m, d = x2.shape
    tm = min(tm, rows_per_batch)
    tpb = rows_per_batch // tm
    return pl.pallas_call(
        _out_proj_kernel,
        grid=(m // tm,),
        in_specs=[pl.BlockSpec((tm, d), lambda i: (i, 0)),
                  pl.BlockSpec((d, d), lambda i: (0, 0), pipeline_mode=pl.Buffered(1)),
                  pl.BlockSpec((tm, d), lambda i: (i, 0)),
                  pl.BlockSpec((1, 1, d), lambda i: (i // tpb, 0, 0))],
        out_specs=pl.BlockSpec((tm, d), lambda i: (i, 0)),
        out_shape=jax.ShapeDtypeStruct((m, d), F32),
        compiler_params=_params("parallel"),
        name="out_proj_residual",
    )(z, w_out, x2, gate)


FFN_SUB = 256
FFN_SUBS = 2
FFN_HALO = 16
FFN_OUT_COLS = 512


def _ffn_kernel(x_ref, xp_ref, xn_ref, shift_ref, scale_ref, gate_ref, gain_ref, wu_ref, cw_ref, cb_ref,
                wd_ref, o_ref, h_ref, *, tpb):
    i = pl.program_id(0)
    kk = pl.program_id(1)
    tm = x_ref.shape[0]
    d = x_ref.shape[1]
    ext = tm + 2 * FFN_HALO

    def modnorm(x):
        ms = jnp.mean(x * x, axis=-1, keepdims=True)
        y = x * lax.rsqrt(ms + EPS) * gain_ref[...]
        return y * (1.0 + scale_ref[0]) + shift_ref[0]

    @pl.when(kk == 0)
    def _():
        keep_prev = jnp.where((i % tpb) == 0, 0.0, 1.0)
        keep_next = jnp.where((i % tpb) == tpb - 1, 0.0, 1.0)
        h_ref[0:FFN_HALO, :] = (modnorm(xp_ref[...]) * keep_prev).astype(BF16)
        h_ref[FFN_HALO:FFN_HALO + tm, :] = modnorm(x_ref[...]).astype(BF16)
        h_ref[FFN_HALO + tm:ext, :] = (modnorm(xn_ref[...]) * keep_next).astype(BF16)
        o_ref[...] = jnp.zeros_like(o_ref)

    w2 = 2 * FFN_SUB
    us = [jnp.dot(h_ref[...], wu_ref[:, s * w2:(s + 1) * w2], preferred_element_type=F32)
          for s in range(FFN_SUBS)]
    for s in range(FFN_SUBS):
        u = us[s]
        cols = slice(s * w2, (s + 1) * w2)
        rows = slice(FFN_HALO, FFN_HALO + tm)
        conv = (cw_ref[0:1, cols] * pltpu.roll(u, 1, axis=0)[rows]
                + cw_ref[1:2, cols] * u[rows]
                + cw_ref[2:3, cols] * pltpu.roll(u, ext - 1, axis=0)[rows] + cb_ref[:, cols])
        a = conv[:, :FFN_SUB]
        act = (a * _sigmoid(a) * conv[:, FFN_SUB:]).astype(BF16)
        for nb in range(d // FFN_OUT_COLS):
            oc = slice(nb * FFN_OUT_COLS, (nb + 1) * FFN_OUT_COLS)
            o_ref[:, oc] += jnp.dot(act, wd_ref[s * FFN_SUB:(s + 1) * FFN_SUB, oc],
                                    preferred_element_type=F32)

    @pl.when(kk == pl.num_programs(1) - 1)
    def _():
        o_ref[...] = x_ref[...] + gate_ref[0] * o_ref[...]


def _conv_ffn(x2, shift, scale, gate, gain, w_up_r, conv_w_r, conv_b_r, w_down, rows_per_batch, tm=1024):
    m, d = x2.shape
    dff = w_down.shape[0]
    tm = min(tm, rows_per_batch)
    tpb = rows_per_batch // tm
    hb = tm // FFN_HALO
    nhalo = m // FFN_HALO
    tk = FFN_SUB * FFN_SUBS
    nk = dff // tk
    once = pl.Buffered(1)
    return pl.pallas_call(
        functools.partial(_ffn_kernel, tpb=tpb),
        grid=(m // tm, nk),
        in_specs=[pl.BlockSpec((tm, d), lambda i, k: (i, 0), pipeline_mode=once),
                  pl.BlockSpec((FFN_HALO, d), lambda i, k: (jnp.maximum(i * hb - 1, 0), 0)),
                  pl.BlockSpec((FFN_HALO, d), lambda i, k: (jnp.minimum((i + 1) * hb, nhalo - 1), 0)),
                  pl.BlockSpec((1, 1, d), lambda i, k: (i // tpb, 0, 0)),
                  pl.BlockSpec((1, 1, d), lambda i, k: (i // tpb, 0, 0)),
                  pl.BlockSpec((1, 1, d), lambda i, k: (i // tpb, 0, 0)),
                  pl.BlockSpec((1, d), lambda i, k: (0, 0)),
                  pl.BlockSpec((d, 2 * tk), lambda i, k: (0, k)),
                  pl.BlockSpec((3, 2 * tk), lambda i, k: (0, k)),
                  pl.BlockSpec((1, 2 * tk), lambda i, k: (0, k)),
                  pl.BlockSpec((tk, d), lambda i, k: (k, 0))],
        out_specs=pl.BlockSpec((tm, d), lambda i, k: (i, 0)),
        out_shape=jax.ShapeDtypeStruct((m, d), F32),
        scratch_shapes=[pltpu.VMEM((tm + 2 * FFN_HALO, d), BF16)],
        compiler_params=_params("parallel", "arbitrary"),
        name="conv_ffn",
    )(x2, x2, x2, shift, scale, gate, gain.reshape(1, d), w_up_r, conv_w_r, conv_b_r, w_down)


def _regroup_ffn_cols(a):
    lead = a.shape[:-1]
    nblk = D_FF // FFN_SUB
    a = a.reshape(*lead, 2, nblk, FFN_SUB)
    return jnp.swapaxes(a, -3, -2).reshape(*lead, 2 * D_FF)


def _rope_tables(n):
    rows = n // GRID_W
    row = jnp.repeat(jnp.arange(rows), GRID_W)
    col = jnp.tile(jnp.arange(GRID_W), rows)
    n_freq = HEAD_DIM // 4
    inv = ROPE_THETA ** (-jnp.arange(n_freq, dtype=F32) / n_freq)
    ang = jnp.concatenate([row[:, None] * inv, col[:, None] * inv], axis=-1)
    cos, sin = jnp.cos(ang), jnp.sin(ang)
    return jnp.concatenate([cos, cos], axis=-1), jnp.concatenate([-sin, sin], axis=-1)


def _split_w_in(w_in):
    names = ("qa", "ka", "va", "qb", "kb", "vb", "rb", "lrf", "lrb", "ga", "gb")
    splits = (ATTN_WIDTH, KV_WIDTH, KV_WIDTH, GLA_K_WIDTH, GLA_K_WIDTH, GLA_V_WIDTH, GLA_V_WIDTH,
              GLA_LOWRANK, GLA_LOWRANK, D_MODEL, D_MODEL)
    start = dict(zip(names, np.cumsum((0,) + splits[:-1]).tolist()))
    width = dict(zip(names, splits))
    bw = STAGE_COLS
    wt = jnp.swapaxes(w_in, 0, 1)

    def starts(group):
        out = []
        for nm in group:
            assert start[nm] % 8 == 0 and width[nm] % bw == 0
            out += [start[nm] + t * bw for t in range(width[nm] // bw)]
        return out

    w_qk = _transpose_cast_rows(wt, starts(("qa", "ka")), bw)
    w_p = _transpose_cast_rows(wt, starts(("qb", "kb", "vb", "rb", "ga", "gb", "va")), bw)
    lr0, nlr = start["lrf"], 2 * GLA_LOWRANK
    pad = jnp.zeros((w_in.shape[0], LANES - nlr), w_in.dtype)
    w_lr = jnp.concatenate([jnp.swapaxes(wt[lr0:lr0 + nlr], 0, 1), pad], axis=-1).astype(BF16)
    return w_qk, w_p, w_lr


def kernel(x, c, ctx, c_ctx, w_mod, b_mod, g_mix, w_in, q_norm, k_norm, attn_sink, w_gate_f, b_gate_f,
           w_gate_b, b_gate_b, gla_norm, w_attn_o, w_gla_o, w_out, g_ffn, w_up, conv_w, conv_b, w_down):
    batch, seq, d = x.shape
    ctx_len = ctx.shape[1]
    assert w_mod.shape[0] == 1, "single-layer kernel"
    assert d == D_MODEL and seq % GLA_CHUNK == 0 and ctx_len % GLA_CHUNK == 0 and batch <= 7

    cc = jnp.zeros((8, d), F32).at[:batch].set(c).at[batch].set(c_ctx)
    mod = _modulation(cc, w_mod[0], b_mod[0]).reshape(8, 6, d)
    mod_x = [mod[:batch, jj][:, None, :] for jj in range(6)]
    mod_c = [mod[batch:batch + 1, jj][:, None, :] for jj in range(6)]

    w_qk, w_p, w_lr = _split_w_in(w_in[0])
    q_fold = HEAD_DIM ** -0.5 * LOG2E
    qk_norm_w = jnp.concatenate([jnp.tile(q_norm[0] * q_fold, N_Q_HEADS), jnp.tile(k_norm[0], N_KV_HEADS)])
    p_scale = jnp.ones((P_WIDTH,), F32).at[P_QB:P_QB + GLA_K_WIDTH].set(GLA_DK ** -0.5)
    cos2, sin2 = _rope_tables(seq)

    x2 = x.reshape(batch * seq, d)
    c2 = ctx.reshape(batch * ctx_len, d)
    qk, lr = _norm_matmul(x2, mod_x[0], mod_x[1], g_mix[0], w_qk, qk_norm_w, mode="qk",
                          rows_per_batch=seq, cos2=cos2, sin2=sin2, w_lr=w_lr, tm=512, tn=QK_WIDTH)
    p = _norm_matmul(x2, mod_x[0], mod_x[1], g_mix[0], w_p, p_scale, mode="plain", rows_per_batch=seq,
                     tn=1536)
    qk_c, lr_c = _norm_matmul(c2, mod_c[0], mod_c[1], g_mix[0], w_qk, qk_norm_w, mode="qk",
                              rows_per_batch=batch * ctx_len, w_lr=w_lr, tm=512, tn=QK_WIDTH)
    p_c = _norm_matmul(c2, mod_c[0], mod_c[1], g_mix[0], w_p, p_scale, mode="plain",
                       rows_per_batch=batch * ctx_len)

    o_attn = _attention(qk, p, qk_c, p_c, attn_sink[0], batch, seq, ctx_len)

    def hi_lo(w):
        hi = w.astype(BF16)
        return jnp.stack([hi, (w - hi.astype(F32)).astype(BF16)])

    wgf = hi_lo(jnp.zeros((LANES, GLA_K_WIDTH), F32).at[:GLA_LOWRANK].set(w_gate_f[0]))
    wgb = hi_lo(jnp.zeros((LANES, GLA_K_WIDTH), F32).at[GLA_LOWRANK:2 * GLA_LOWRANK].set(w_gate_b[0]))
    bgf = b_gate_f[0].reshape(1, GLA_K_WIDTH)
    bgb = b_gate_b[0].reshape(1, GLA_K_WIDTH)
    zero_state = jnp.zeros((batch, GLA_HEADS, GLA_DV, GLA_DK), F32)
    _, _, sf, sb = _gla(p_c, lr_c, wgf, bgf, wgb, bgb, zero_state, zero_state, batch, ctx_len)
    o_f, o_b, _, _ = _gla(p, lr, wgf, bgf, wgb, bgb, sf, sb, batch, seq)

    z = _merge(o_attn, o_f, o_b, p, gla_norm[0], w_attn_o[0].astype(BF16), w_gla_o[0].astype(BF16))
    x1 = _out_proj(z, w_out[0].astype(BF16), x2, mod_x[2], seq)

    nblk = D_FF // FFN_SUB
    w_up_r = _permute_cast_cols(w_up[0], [(j % 2) * nblk + j // 2 for j in range(2 * nblk)], FFN_SUB)
    out = _conv_ffn(x1, mod_x[3], mod_x[4], mod_x[5], g_ffn[0], w_up_r,
                    _regroup_ffn_cols(conv_w[0]), _regroup_ffn_cols(conv_b[0]).reshape(1, 2 * D_FF),
                    w_down[0].astype(BF16), seq)
    return out.reshape(batch, seq, d)
```
